```python
import jax, jax.numpy as jnp
from jax import lax
import numpy as np

D_MODEL = 2048
BATCH = 4
SEQ = 4096
DEPTH = 4

CHUNK = 64
N_MIXERS = 2
N_A_LAYERS = (DEPTH + 1) // 2
N_B_LAYERS = DEPTH // 2
N_SUB = 3
D_FF = 5632
A_BLOCK = 128
A_EXPAND = 3
A_WIDTH = A_EXPAND * D_MODEL
A_GROUPS = 8
A_GROUP_DIM = A_WIDTH // A_GROUPS
B_HEADS = 16
B_KDIM = D_MODEL // B_HEADS
B_VDIM = D_MODEL // B_HEADS
EPS = 1e-6

kernel_name = "hybrid_gmlp_hgrn2_macaron_adaln"


def rms_norm(x, g):
    x32 = x.astype(jnp.float32)
    y = x32 * lax.rsqrt(jnp.mean(x32 * x32, axis=-1, keepdims=True) + EPS)
    return (y * g.astype(jnp.float32)).astype(x.dtype)


def layer_norm(x, g, b):
    x32 = x.astype(jnp.float32)
    mu = jnp.mean(x32, axis=-1, keepdims=True)
    xc = x32 - mu
    y = xc * lax.rsqrt(jnp.mean(xc * xc, axis=-1, keepdims=True) + EPS)
    return (y * g.astype(jnp.float32) + b.astype(jnp.float32)).astype(x.dtype)


def sublayer_input(x, g, shift, scale):
    return rms_norm(x, g) * (1.0 + scale[:, None, :]) + shift[:, None, :]


def swiglu(h, w1, w3, w2):
    return (jax.nn.silu(h @ w1) * (h @ w3)) @ w2


def gmlp_mixer(h, w_in, ln_g, ln_b, w_s, b_s, w_out):
    B, S, _ = h.shape
    z = jax.nn.gelu(h @ w_in, approximate=False)
    u, v = jnp.split(z, 2, axis=-1)
    v = layer_norm(v, ln_g, ln_b)
    chunk_id = jnp.arange(A_BLOCK) // CHUNK
    mask = chunk_id[None, :] <= chunk_id[:, None]
    w = jnp.where(mask[None], w_s, 0.0)
    v = v.reshape(B, S // A_BLOCK, A_BLOCK, A_GROUPS, A_GROUP_DIM)
    mixed = jnp.einsum('gts,bnsgc->bntgc', w, v) + b_s.T[None, None, :, :, None]
    mixed = mixed.reshape(B, S, A_WIDTH)
    return (u * mixed) @ w_out


def chunk_gated_recurrence(q, k, v, log_f):
    B, S = q.shape[0], q.shape[1]
    n_chunks = S // CHUNK

    def to_chunks(t):
        return t.reshape(B, n_chunks, CHUNK, B_HEADS, t.shape[-1]).transpose(1, 0, 3, 2, 4)

    qc, kc, vc, gc = to_chunks(q), to_chunks(k), to_chunks(v), to_chunks(log_f)
    G = jnp.cumsum(gc, axis=3)
    causal = jnp.tril(jnp.ones((CHUNK, CHUNK), dtype=bool))

    def step(state, xs):
        q_, k_, v_, G_ = xs
        inter = jnp.einsum('bhtk,bhkv->bhtv', q_ * jnp.exp(G_), state)
        diff = G_[:, :, :, None, :] - G_[:, :, None, :, :]
        decay = jnp.exp(jnp.where(causal[:, :, None], diff, -jnp.inf))
        scores = jnp.einsum('bhtk,bhsk,bhtsk->bhts', q_, k_, decay)
        intra = jnp.einsum('bhts,bhsv->bhtv', scores, v_)
        G_last = G_[:, :, -1:, :]
        new_state = (jnp.exp(G_last[:, :, 0, :, None]) * state
                     + jnp.einsum('bhsk,bhsv->bhkv', k_ * jnp.exp(G_last - G_), v_))
        return new_state, inter + intra

    s0 = jnp.zeros((B, B_HEADS, B_KDIM, B_VDIM), jnp.float32)
    _, out = lax.scan(step, s0, (qc, kc, vc, G))
    return out.transpose(1, 0, 3, 2, 4).reshape(B, S, B_HEADS, B_VDIM)


def hgrn2_mixer(h, w_in, lb, norm_g, w_out):
    B, S, _ = h.shape
    q, f_logit, i, g = jnp.split(h @ w_in, 4, axis=-1)
    q = jax.nn.silu(q)
    log_f = jnp.logaddexp(jnp.log(lb), jnp.log1p(-lb) + jax.nn.log_sigmoid(f_logit.astype(jnp.float32)))
    k = -jnp.expm1(log_f)
    heads = lambda t: t.reshape(B, S, B_HEADS, -1).astype(jnp.float32)
    o = chunk_gated_recurrence(heads(q), heads(k), heads(i), heads(log_f))
    o = rms_norm(o, norm_g).reshape(B, S, D_MODEL).astype(h.dtype)
    return (o * jax.nn.silu(g)) @ w_out


def setup_inputs(seed: int = 0) -> dict:
    key = jax.random.key(seed)
    ks = jax.random.split(key, 20)
    f32 = jnp.float32
    nrm = lambda k, shape, s: jax.random.normal(k, shape, f32) * s
    return {
        "x": nrm(ks[0], (BATCH, SEQ, D_MODEL), 1.0),
        "c": nrm(ks[1], (BATCH, D_MODEL), 1.0),
        "mod_w": nrm(ks[2], (DEPTH, D_MODEL, N_SUB * 3 * D_MODEL), 0.5 * D_MODEL ** -0.5),
        "mod_b": nrm(ks[3], (DEPTH, N_SUB * 3 * D_MODEL), 0.02),
        "norm_g": 1.0 + nrm(ks[4], (DEPTH, N_SUB, D_MODEL), 0.02),
        "ffn_w1": nrm(ks[5], (DEPTH, 2, D_MODEL, D_FF), D_MODEL ** -0.5),
        "ffn_w3": nrm(ks[6], (DEPTH, 2, D_MODEL, D_FF), D_MODEL ** -0.5),
        "ffn_w2": nrm(ks[7], (DEPTH, 2, D_FF, D_MODEL), D_FF ** -0.5),
        "a_w_in": nrm(ks[8], (N_A_LAYERS, D_MODEL, 2 * A_WIDTH), D_MODEL ** -0.5),
        "a_ln_g": 1.0 + nrm(ks[9], (N_A_LAYERS, A_WIDTH), 0.02),
        "a_ln_b": nrm(ks[10], (N_A_LAYERS, A_WIDTH), 0.02),
        "a_w_s": nrm(ks[11], (N_A_LAYERS, A_GROUPS, A_BLOCK, A_BLOCK), A_BLOCK ** -0.5),
        "a_b_s": 1.0 + nrm(ks[12], (N_A_LAYERS, A_GROUPS, A_BLOCK), 0.1),
        "a_w_out": nrm(ks[13], (N_A_LAYERS, A_WIDTH, D_MODEL), A_WIDTH ** -0.5),
        "b_w_in": nrm(ks[14], (N_B_LAYERS, D_MODEL, 4 * D_MODEL), D_MODEL ** -0.5),
        "b_lb_logits": nrm(ks[15], (N_B_LAYERS, D_MODEL), 1.0),
        "b_norm_g": 1.0 + nrm(ks[16], (N_B_LAYERS, B_HEADS, B_VDIM), 0.02),
        "b_w_out": nrm(ks[17], (N_B_LAYERS, D_MODEL, D_MODEL), D_MODEL ** -0.5),
        "final_g": 1.0 + nrm(ks[18], (D_MODEL,), 0.02),
    }


def reference(x, c, mod_w, mod_b, norm_g, ffn_w1, ffn_w3, ffn_w2, a_w_in, a_ln_g, a_ln_b, a_w_s, a_b_s, a_w_out, b_w_in, b_lb_logits, b_norm_g, b_w_out, final_g):
    lb_all = jnp.cumsum(jax.nn.softmax(b_lb_logits.astype(jnp.float32), axis=0), axis=0)
    lb_all = jnp.maximum(lb_all - lb_all[:1], 0.0)
    cond = jax.nn.silu(c)
    for layer in range(DEPTH):
        mod = (cond @ mod_w[layer] + mod_b[layer]).reshape(-1, N_SUB, 3, D_MODEL)
        shift, scale, gate = mod[:, :, 0], mod[:, :, 1], mod[:, :, 2]
        h = sublayer_input(x, norm_g[layer, 0], shift[:, 0], scale[:, 0])
        x = x + 0.5 * gate[:, 0][:, None, :] * swiglu(h, ffn_w1[layer, 0], ffn_w3[layer, 0], ffn_w2[layer, 0])
        h = sublayer_input(x, norm_g[layer, 1], shift[:, 1], scale[:, 1])
        j = layer // N_MIXERS
        if layer % N_MIXERS == 0:
            y = gmlp_mixer(h, a_w_in[j], a_ln_g[j], a_ln_b[j], a_w_s[j], a_b_s[j], a_w_out[j])
        else:
            y = hgrn2_mixer(h, b_w_in[j], lb_all[j], b_norm_g[j], b_w_out[j])
        x = x + gate[:, 1][:, None, :] * y
        h = sublayer_input(x, norm_g[layer, 2], shift[:, 2], scale[:, 2])
        x = x + 0.5 * gate[:, 2][:, None, :] * swiglu(h, ffn_w1[layer, 1], ffn_w3[layer, 1], ffn_w2[layer, 1])
    return rms_norm(x, final_g)
```

```python
import functools

import numpy as np
import jax
import jax.numpy as jnp
from jax import lax
from jax.experimental import pallas as pl
from jax.experimental.pallas import tpu as pltpu

EPS = 1e-6
CHUNK = 64
A_BLOCK = 128
N_SUB = 3
LANES = 128
SUBLANES = 8
VMEM_LIMIT_BYTES = 56 * 1024 * 1024

F32 = jnp.float32
BF16 = jnp.bfloat16


def _tile(dim, target, align):
    if dim <= target:
        return dim
    t = (target // align) * align
    while t >= align:
        if dim % t == 0:
            return t
        t -= align
    return dim


def _params(*semantics):
    return pltpu.CompilerParams(dimension_semantics=semantics, vmem_limit_bytes=VMEM_LIMIT_BYTES)


def _dot(a, b):
    return jnp.dot(a, b, preferred_element_type=F32)


def _dot_nt(a, b):
    return lax.dot_general(a, b, (((1,), (1,)), ((), ())), preferred_element_type=F32)


def _dot_tn(a, b):
    return lax.dot_general(a, b, (((0,), (0,)), ((), ())), preferred_element_type=F32)


def _silu(x):
    return x * jax.nn.sigmoid(x)


def _rms(x):
    return x * lax.rsqrt(jnp.mean(x * x, axis=-1, keepdims=True) + EPS)


def _norm_mod(x, g, shift, scale):
    return _rms(x) * g * (1.0 + scale) + shift


def _mod_kernel(c_ref, w_ref, b_ref, o_ref):
    o_ref[...] = _dot(_silu(c_ref[...]), w_ref[...]) + b_ref[...]


def _mod_all(c, mod_w, mod_b):
    L, D, N = mod_w.shape
    B = c.shape[0]
    Bp = -(-B // SUBLANES) * SUBLANES
    c_pad = jnp.zeros((Bp, D), F32).at[:B].set(c)
    tn = _tile(N, 1024, LANES)
    return pl.pallas_call(
        _mod_kernel,
        grid=(L, N // tn),
        in_specs=[
            pl.BlockSpec((Bp, D), lambda l, j: (0, 0)),
            pl.BlockSpec((None, D, tn), lambda l, j: (l, 0, j)),
            pl.BlockSpec((None, 1, tn), lambda l, j: (l, 0, j)),
        ],
        out_specs=pl.BlockSpec((None, Bp, tn), lambda l, j: (l, 0, j)),
        out_shape=jax.ShapeDtypeStruct((L, Bp, N), F32),
        compiler_params=_params("parallel", "parallel"),
        name="mod",
    )(c_pad, mod_w, mod_b.reshape(L, 1, N))


def _ffn_kernel(x_ref, mod_ref, g_ref, w1_ref, w3_ref, w2_ref, fg_ref, o_ref, h_ref, *, final_norm):
    j = pl.program_id(1)

    @pl.when(j == 0)
    def _():
        x = x_ref[...]
        h_ref[...] = _norm_mod(x, g_ref[...], mod_ref[0:1, :], mod_ref[1:2, :]).astype(BF16)
        o_ref[...] = x

    h = h_ref[...]
    a = _dot(h, w1_ref[...])
    b = _dot(h, w3_ref[...])
    p = (_silu(a) * b).astype(BF16)
    o_ref[...] += (0.5 * mod_ref[2:3, :]) * _dot(p, w2_ref[...])

    if final_norm:
        @pl.when(j == pl.num_programs(1) - 1)
        def _():
            o_ref[...] = _rms(o_ref[...]) * fg_ref[...]


def _ffn(x, mod3, g, w1, w3, w2, final_g, seq, final_norm):
    M, D = x.shape
    Fd = w1.shape[1]
    tm = _tile(seq, 512, SUBLANES)
    tf = _tile(Fd, 512, LANES)
    per_b = seq // tm
    return pl.pallas_call(
        functools.partial(_ffn_kernel, final_norm=final_norm),
        grid=(M // tm, Fd // tf),
        in_specs=[
            pl.BlockSpec((tm, D), lambda i, j: (i, 0)),
            pl.BlockSpec((None, 3, D), lambda i, j: (i // per_b, 0, 0)),
            pl.BlockSpec((1, D), lambda i, j: (0, 0)),
            pl.BlockSpec((D, tf), lambda i, j: (0, j)),
            pl.BlockSpec((D, tf), lambda i, j: (0, j)),
            pl.BlockSpec((tf, D), lambda i, j: (j, 0)),
            pl.BlockSpec((1, D), lambda i, j: (0, 0)),
        ],
        out_specs=pl.BlockSpec((tm, D), lambda i, j: (i, 0)),
        out_shape=jax.ShapeDtypeStruct((M, D), F32),
        scratch_shapes=[pltpu.VMEM((tm, D), BF16)],
        compiler_params=_params("parallel", "arbitrary"),
        name="ffn",
    )(x, mod3, g.reshape(1, D), w1, w3, w2, final_g.reshape(1, D))


def _gelu(x):
    return 0.5 * x * (1.0 + lax.erf(x * np.float32(1.0 / np.sqrt(2.0))))


def _inproj_a_kernel(x_ref, mod_ref, g_ref, w_ref, o_ref, h_ref):
    @pl.when(pl.program_id(1) == 0)
    def _():
        h_ref[...] = _norm_mod(x_ref[...], g_ref[...], mod_ref[0:1, :], mod_ref[1:2, :]).astype(BF16)

    o_ref[...] = _gelu(_dot(h_ref[...], w_ref[...])).astype(o_ref.dtype)


def _inproj_a(x, mod3, g, w, seq):
    M, D = x.shape
    N = w.shape[1]
    tm = _tile(seq, 1024, SUBLANES)
    tn = _tile(N, 512, LANES)
    per_b = seq // tm
    return pl.pallas_call(
        _inproj_a_kernel,
        grid=(M // tm, N // tn),
        in_specs=[
            pl.BlockSpec((tm, D), lambda i, j: (i, 0)),
            pl.BlockSpec((None, 3, D), lambda i, j: (i // per_b, 0, 0)),
            pl.BlockSpec((1, D), lambda i, j: (0, 0)),
            pl.BlockSpec((D, tn), lambda i, j: (0, j)),
        ],
        out_specs=pl.BlockSpec((tm, tn), lambda i, j: (i, j)),
        out_shape=jax.ShapeDtypeStruct((M, N), BF16),
        scratch_shapes=[pltpu.VMEM((tm, D), BF16)],
        compiler_params=_params("parallel", "arbitrary"),
        name="inproj_a",
    )(x, mod3, g.reshape(1, D), w)


def _log_forget(x, lbl, row):
    m = jnp.max(lbl, axis=0, keepdims=True)
    e = jnp.exp(lbl - m)
    sm = e / jnp.sum(e, axis=0, keepdims=True)
    cum = sm[0:1]
    for r in range(1, row + 1):
        cum = cum + sm[r:r + 1]
    lb = jnp.maximum(cum - sm[0:1], 0.0)
    log_sig = jnp.minimum(x, 0.0) - jnp.log1p(jnp.exp(-jnp.abs(x)))
    a = jnp.log(lb)
    b = jnp.log1p(-lb) + log_sig
    hi = jnp.maximum(a, b)
    lo = jnp.minimum(a, b)
    return hi + jnp.log1p(jnp.exp(lo - hi))


def _inproj_b_kernel(x_ref, mod_ref, g_ref, w_ref, lbl_ref, o_ref, h_ref, *, row, per_sec):
    j = pl.program_id(1)

    @pl.when(j == 0)
    def _():
        h_ref[...] = _norm_mod(x_ref[...], g_ref[...], mod_ref[0:1, :], mod_ref[1:2, :]).astype(BF16)

    y = _dot(h_ref[...], w_ref[...])
    sec = j // per_sec

    @pl.when(jnp.logical_or(sec == 0, sec == 3))
    def _():
        o_ref[...] = _silu(y)

    @pl.when(sec == 1)
    def _():
        o_ref[...] = _log_forget(y, lbl_ref[...], row)

    @pl.when(sec == 2)
    def _():
        o_ref[...] = y


def _inproj_b(x, mod3, g, w, lb_logits, row, seq):
    M, D = x.shape
    N = w.shape[1]
    R = lb_logits.shape[0]
    tm = _tile(seq, 1024, SUBLANES)
    tn = _tile(D, 512, LANES)
    per_sec = D // tn
    per_b = seq // tm
    return pl.pallas_call(
        functools.partial(_inproj_b_kernel, row=row, per_sec=per_sec),
        grid=(M // tm, N // tn),
        in_specs=[
            pl.BlockSpec((tm, D), lambda i, j: (i, 0)),
            pl.BlockSpec((None, 3, D), lambda i, j: (i // per_b, 0, 0)),
            pl.BlockSpec((1, D), lambda i, j: (0, 0)),
            pl.BlockSpec((D, tn), lambda i, j: (0, j)),
            pl.BlockSpec((R, tn), lambda i, j: (0, j % per_sec)),
        ],
        out_specs=pl.BlockSpec((tm, tn), lambda i, j: (i, j)),
        out_shape=jax.ShapeDtypeStruct((M, N), F32),
        scratch_shapes=[pltpu.VMEM((tm, D), BF16)],
        compiler_params=_params("parallel", "arbitrary"),
        name="inproj_b",
    )(x, mod3, g.reshape(1, D), w, lb_logits)


def _sgu_kernel(u_ref, v_ref, lng_ref, lnb_ref, ws_ref, bs_ref, o_ref, *, groups):
    tm, W = u_ref.shape
    gd = W // groups
    t_id = lax.broadcasted_iota(jnp.int32, (A_BLOCK, A_BLOCK), 0) // CHUNK
    s_id = lax.broadcasted_iota(jnp.int32, (A_BLOCK, A_BLOCK), 1) // CHUNK
    causal = s_id <= t_id
    lng = lng_ref[...]
    lnb = lnb_ref[...]
    for r in range(tm // A_BLOCK):
        rows = pl.ds(r * A_BLOCK, A_BLOCK)
        v = v_ref[rows, :].astype(F32)
        mu = jnp.mean(v, axis=-1, keepdims=True)
        vc = v - mu
        vn = vc * lax.rsqrt(jnp.mean(vc * vc, axis=-1, keepdims=True) + EPS) * lng + lnb
        vn = vn.astype(BF16)
        for g in range(groups):
            cols = pl.ds(g * gd, gd)
            w = jnp.where(causal, ws_ref[g], 0.0).astype(BF16)
            mixed = _dot(w, vn[:, g * gd:(g + 1) * gd]) + bs_ref[:, g:g + 1]
            o_ref[rows, cols] = (u_ref[rows, cols].astype(F32) * mixed).astype(o_ref.dtype)


def _sgu(z, ln_g, ln_b, w_s, b_s):
    M, W2 = z.shape
    W = W2 // 2
    G = w_s.shape[0]
    tm = 2 * A_BLOCK if M % (2 * A_BLOCK) == 0 else A_BLOCK
    return pl.pallas_call(
        functools.partial(_sgu_kernel, groups=G),
        grid=(M // tm,),
        in_specs=[
            pl.BlockSpec((tm, W), lambda i: (i, 0)),
            pl.BlockSpec((tm, W), lambda i: (i, 1)),
            pl.BlockSpec((1, W), lambda i: (0, 0)),
            pl.BlockSpec((1, W), lambda i: (0, 0)),
            pl.BlockSpec((G, A_BLOCK, A_BLOCK), lambda i: (0, 0, 0)),
            pl.BlockSpec((A_BLOCK, G), lambda i: (0, 0)),
        ],
        out_specs=pl.BlockSpec((tm, W), lambda i: (i, 0)),
        out_shape=jax.ShapeDtypeStruct((M, W), BF16),
        compiler_params=_params("parallel"),
        name="sgu",
    )(z, z, ln_g.reshape(1, W), ln_b.reshape(1, W), w_s, b_s.T)


def _level_sizes():
    sizes, n = [], CHUNK // 2
    while n >= 1:
        sizes.append(n)
        n //= 2
    return sizes


def _decay_tables():
    L = CHUNK
    sizes = _level_sizes()
    D = np.zeros((2 + len(sizes), L, L), np.float32)
    lid = np.full((L, L), -1, np.int32)
    r = np.arange(L)
    for t in range(L):
        D[0, t] = r <= t
        D[1, t] = r > t
        lid[t, t] = 0
    for li, n in enumerate(sizes):
        for t in range(L):
            m = (t // (2 * n)) * 2 * n + n - 1
            if (t // n) % 2 == 1:
                D[2 + li, t] = (r > m) & (r <= t)
                lid[t, m - n + 1:m + 1] = li + 1
            else:
                D[2 + li, t] = (r > t) & (r <= m)
    D = D.reshape(-1, L)
    return np.concatenate([D, D, D], axis=1), lid


def _split3(x):
    hi = x.astype(BF16)
    r1 = x - hi.astype(F32)
    mid = r1.astype(BF16)
    lo = (r1 - mid.astype(F32)).astype(BF16)
    return jnp.concatenate([hi, mid, lo], axis=0)


def _recur_kernel(q_ref, lf_ref, v_ref, g_ref, ng_ref, dcat_ref, lid_ref, o_ref, s_ref):
    @pl.when(pl.program_id(2) == 0)
    def _():
        s_ref[...] = jnp.zeros_like(s_ref)

    ts, dk = q_ref.shape
    n_lvl = len(_level_sizes())
    dcat = dcat_ref[...]
    lid = lid_ref[...]
    ones = jnp.ones((3 * CHUNK, dk), BF16)
    ng = ng_ref[...]
    state = s_ref[...]
    for c in range(ts // CHUNK):
        rows = pl.ds(c * CHUNK, CHUNK)
        q = q_ref[rows, :]
        lf = lf_ref[rows, :]
        v = v_ref[rows, :]
        k = 1.0 - jnp.exp(lf)
        lf3 = _split3(lf)
        e = jnp.exp(_dot(dcat, lf3))
        vb = v.astype(BF16)
        qs = (q * e[0:CHUNK]).astype(BF16)
        ks = (k * e[CHUNK:2 * CHUNK]).astype(BF16)
        inter = _dot(qs, state.astype(BF16))
        sc = jnp.where(lid == 0, _dot_nt(q.astype(BF16), k.astype(BF16)), 0.0)
        for li in range(n_lvl):
            el = e[(2 + li) * CHUNK:(3 + li) * CHUNK]
            sc = jnp.where(lid == li + 1, _dot_nt((q * el).astype(BF16), (k * el).astype(BF16)), sc)
        o = inter + _dot(sc.astype(BF16), vb)
        g_last = _dot_tn(lf3, ones)
        state = state * jnp.exp(g_last) + _dot_tn(ks, vb)
        o = _rms(o) * ng * g_ref[rows, :]
        o_ref[rows, :] = o.astype(o_ref.dtype)
    s_ref[...] = state


def _recurrence(z4, norm_g, batch, seq):
    M, N = z4.shape
    D = N // 4
    H, dv = norm_g.shape
    ts = _tile(seq, 512, CHUNK)
    nt = seq // ts
    dcat, lid = _decay_tables()

    def spec(sec):
        return pl.BlockSpec((ts, dv), lambda b, h, t: (b * nt + t, sec * H + h))

    return pl.pallas_call(
        _recur_kernel,
        grid=(batch, H, nt),
        in_specs=[
            spec(0), spec(1), spec(2), spec(3),
            pl.BlockSpec((None, 1, dv), lambda b, h, t: (h, 0, 0)),
            pl.BlockSpec(dcat.shape, lambda b, h, t: (0, 0)),
            pl.BlockSpec(lid.shape, lambda b, h, t: (0, 0)),
        ],
        out_specs=pl.BlockSpec((ts, dv), lambda b, h, t: (b * nt + t, h)),
        out_shape=jax.ShapeDtypeStruct((M, D), BF16),
        scratch_shapes=[pltpu.VMEM((dv, dv), F32)],
        compiler_params=_params("parallel", "parallel", "arbitrary"),
        name="recurrence",
    )(z4, z4, z4, z4, norm_g.reshape(H, 1, dv), jnp.asarray(dcat, BF16), jnp.asarray(lid))


def _outproj_kernel(p_ref, w_ref, x_ref, mod_ref, o_ref):
    o_ref[...] = x_ref[...] + mod_ref[2:3, :] * _dot(p_ref[...], w_ref[...])


def _outproj(p, w, x, mod3, seq):
    M, K = p.shape
    D = w.shape[1]
    tm = _tile(seq, 512, SUBLANES)
    tn = _tile(D, 512, LANES)
    per_b = seq // tm
    return pl.pallas_call(
        _outproj_kernel,
        grid=(M // tm, D // tn),
        in_specs=[
            pl.BlockSpec((tm, K), lambda i, j: (i, 0)),
            pl.BlockSpec((K, tn), lambda i, j: (0, j)),
            pl.BlockSpec((tm, tn), lambda i, j: (i, j)),
            pl.BlockSpec((None, 3, tn), lambda i, j: (i // per_b, 0, j)),
        ],
        out_specs=pl.BlockSpec((tm, tn), lambda i, j: (i, j)),
        out_shape=jax.ShapeDtypeStruct((M, D), F32),
        compiler_params=_params("parallel", "parallel"),
        name="outproj",
    )(p, w, x, mod3)


def kernel(x, c, mod_w, mod_b, norm_g, ffn_w1, ffn_w3, ffn_w2, a_w_in, a_ln_g, a_ln_b, a_w_s, a_b_s, a_w_out,
           b_w_in, b_lb_logits, b_norm_g, b_w_out, final_g):
    B, S, D = x.shape
    depth = mod_w.shape[0]
    n_mixers = 2
    assert S % A_BLOCK == 0 and A_BLOCK % CHUNK == 0 and D % LANES == 0
    bf = lambda w: w.astype(BF16)

    mod = _mod_all(c, mod_w, mod_b)[:, :B].reshape(depth, B, N_SUB, 3, D)
    xf = x.reshape(B * S, D)
    for layer in range(depth):
        j = layer // n_mixers
        last = layer == depth - 1
        xf = _ffn(xf, mod[layer, :, 0], norm_g[layer, 0], bf(ffn_w1[layer, 0]), bf(ffn_w3[layer, 0]),
                  bf(ffn_w2[layer, 0]), final_g, S, False)
        if layer % n_mixers == 0:
            z = _inproj_a(xf, mod[layer, :, 1], norm_g[layer, 1], bf(a_w_in[j]), S)
            p = _sgu(z, a_ln_g[j], a_ln_b[j], a_w_s[j], a_b_s[j])
            xf = _outproj(p, bf(a_w_out[j]), xf, mod[layer, :, 1], S)
        else:
            z4 = _inproj_b(xf, mod[layer, :, 1], norm_g[layer, 1], bf(b_w_in[j]), b_lb_logits, j, S)
            p = _recurrence(z4, b_norm_g[j], B, S)
            xf = _outproj(p, bf(b_w_out[j]), xf, mod[layer, :, 1], S)
        xf = _ffn(xf, mod[layer, :, 2], norm_g[layer, 2], bf(ffn_w1[layer, 1]), bf(ffn_w3[layer, 1]),
                  bf(ffn_w2[layer, 1]), final_g, S, last)
    return xf.reshape(B, S, D)
```

```python
import functools

import numpy as np
import jax
import jax.numpy as jnp
from jax import lax
from jax.experimental import pallas as pl
from jax.experimental.pallas import tpu as pltpu

EPS = 1e-6
CHUNK = 64
A_BLOCK = 128
N_SUB = 3
LANES = 128
SUBLANES = 8
MXU_WIDTH = 256
VMEM_LIMIT_BYTES = 60 * 1024 * 1024

F32 = jnp.float32
BF16 = jnp.bfloat16


def _tile(dim, target, align):
    if dim <= target:
        return dim
    t = (target // align) * align
    while t >= align:
        if dim % t == 0:
            return t
        t -= align
    return dim


def _params(*semantics):
    return pltpu.CompilerParams(dimension_semantics=semantics, vmem_limit_bytes=VMEM_LIMIT_BYTES)


def _dot(a, b):
    return jnp.dot(a, b, preferred_element_type=F32)


def _dot_nt(a, b):
    return lax.dot_general(a, b, (((1,), (1,)), ((), ())), preferred_element_type=F32)


def _dot_tn(a, b):
    return lax.dot_general(a, b, (((0,), (0,)), ((), ())), preferred_element_type=F32)


def _silu(x):
    return x * jax.nn.sigmoid(x)


def _rms(x):
    return x * lax.rsqrt(jnp.mean(x * x, axis=-1, keepdims=True) + EPS)


def _norm_mod(x, g, shift, scale):
    return _rms(x) * g * (1.0 + scale) + shift


def _mod_kernel(c_ref, w_ref, b_ref, o_ref):
    o_ref[...] = _dot(_silu(c_ref[...]), w_ref[...]) + b_ref[...]


def _mod_all(c, mod_w, mod_b):
    L, D, N = mod_w.shape
    B = c.shape[0]
    Bp = -(-B // SUBLANES) * SUBLANES
    c_pad = jnp.zeros((Bp, D), F32).at[:B].set(c)
    tn = _tile(N, 1024, LANES)
    return pl.pallas_call(
        _mod_kernel,
        grid=(L, N // tn),
        in_specs=[
            pl.BlockSpec((Bp, D), lambda l, j: (0, 0)),
            pl.BlockSpec((None, D, tn), lambda l, j: (l, 0, j)),
            pl.BlockSpec((None, 1, tn), lambda l, j: (l, 0, j)),
        ],
        out_specs=pl.BlockSpec((None, Bp, tn), lambda l, j: (l, 0, j)),
        out_shape=jax.ShapeDtypeStruct((L, Bp, N), F32),
        compiler_params=_params("parallel", "parallel"),
        name="mod",
    )(c_pad, mod_w, mod_b.reshape(L, 1, N))


def _ffn_kernel(x_ref, mod_ref, g_ref, w1_ref, w3_ref, w2_ref, fg_ref, o_ref, h_ref, *, final_norm):
    j = pl.program_id(1)

    @pl.when(j == 0)
    def _():
        x = x_ref[...]
        h_ref[...] = _norm_mod(x, g_ref[...], mod_ref[0:1, :], mod_ref[1:2, :]).astype(BF16)
        o_ref[...] = x

    h = h_ref[...]
    a = _dot(h, w1_ref[...])
    b = _dot(h, w3_ref[...])
    p = (_silu(a) * b).astype(BF16)
    o_ref[...] += (0.5 * mod_ref[2:3, :]) * _dot(p, w2_ref[...])

    if final_norm:
        @pl.when(j == pl.num_programs(1) - 1)
        def _():
            o_ref[...] = _rms(o_ref[...]) * fg_ref[...]


def _ffn(x, mod3, g, w1, w3, w2, layer, half, final_g, seq, final_norm):
    M, D = x.shape
    Fd = w1.shape[-1]
    tm = _tile(seq, 512, SUBLANES)
    tf = _tile(Fd, 512, LANES)
    per_b = seq // tm
    return pl.pallas_call(
        functools.partial(_ffn_kernel, final_norm=final_norm),
        grid=(M // tm, Fd // tf),
        in_specs=[
            pl.BlockSpec((tm, D), lambda i, j: (i, 0)),
            pl.BlockSpec((None, 3, D), lambda i, j: (i // per_b, 0, 0)),
            pl.BlockSpec((1, D), lambda i, j: (0, 0)),
            pl.BlockSpec((None, None, D, tf), lambda i, j: (layer, half, 0, j)),
            pl.BlockSpec((None, None, D, tf), lambda i, j: (layer, half, 0, j)),
            pl.BlockSpec((None, None, tf, D), lambda i, j: (layer, half, j, 0)),
            pl.BlockSpec((1, D), lambda i, j: (0, 0)),
        ],
        out_specs=pl.BlockSpec((tm, D), lambda i, j: (i, 0)),
        out_shape=jax.ShapeDtypeStruct((M, D), F32),
        scratch_shapes=[pltpu.VMEM((tm, D), BF16)],
        compiler_params=_params("parallel", "arbitrary"),
        name="ffn",
    )(x, mod3, g.reshape(1, D), w1, w3, w2, final_g.reshape(1, D))


def _gelu(x):
    return 0.5 * x * (1.0 + lax.erf(x * np.float32(1.0 / np.sqrt(2.0))))


def _inproj_a_kernel(x_ref, mod_ref, g_ref, w_ref, o_ref, h_ref):
    @pl.when(pl.program_id(1) == 0)
    def _():
        h_ref[...] = _norm_mod(x_ref[...], g_ref[...], mod_ref[0:1, :], mod_ref[1:2, :]).astype(BF16)

    o_ref[...] = _gelu(_dot(h_ref[...], w_ref[...])).astype(o_ref.dtype)


def _inproj_a(x, mod3, g, w, idx, seq):
    M, D = x.shape
    N = w.shape[-1]
    tm = _tile(seq, 1024, SUBLANES)
    tn = _tile(N, 1024, LANES)
    per_b = seq // tm
    return pl.pallas_call(
        _inproj_a_kernel,
        grid=(M // tm, N // tn),
        in_specs=[
            pl.BlockSpec((tm, D), lambda i, j: (i, 0)),
            pl.BlockSpec((None, 3, D), lambda i, j: (i // per_b, 0, 0)),
            pl.BlockSpec((1, D), lambda i, j: (0, 0)),
            pl.BlockSpec((None, D, tn), lambda i, j: (idx, 0, j)),
        ],
        out_specs=pl.BlockSpec((tm, tn), lambda i, j: (i, j)),
        out_shape=jax.ShapeDtypeStruct((M, N), BF16),
        scratch_shapes=[pltpu.VMEM((tm, D), BF16)],
        compiler_params=_params("parallel", "arbitrary"),
        name="inproj_a",
    )(x, mod3, g.reshape(1, D), w)


def _log_forget(x, lbl, row):
    m = jnp.max(lbl, axis=0, keepdims=True)
    e = jnp.exp(lbl - m)
    sm = e / jnp.sum(e, axis=0, keepdims=True)
    cum = sm[0:1]
    for r in range(1, row + 1):
        cum = cum + sm[r:r + 1]
    lb = jnp.maximum(cum - sm[0:1], 0.0)
    log_sig = jnp.minimum(x, 0.0) - jnp.log1p(jnp.exp(-jnp.abs(x)))
    a = jnp.log(lb)
    b = jnp.log1p(-lb) + log_sig
    hi = jnp.maximum(a, b)
    lo = jnp.minimum(a, b)
    return hi + jnp.log1p(jnp.exp(lo - hi))


def _inproj_b_kernel(x_ref, mod_ref, g_ref, wq_ref, wf_ref, wi_ref, wg_ref, lbl_ref,
                     q_ref, lf_ref, i_ref, sg_ref, h_ref, *, row):
    @pl.when(pl.program_id(1) == 0)
    def _():
        h_ref[...] = _norm_mod(x_ref[...], g_ref[...], mod_ref[0:1, :], mod_ref[1:2, :]).astype(BF16)

    h = h_ref[...]
    q_ref[...] = _silu(_dot(h, wq_ref[...])).astype(q_ref.dtype)
    lf_ref[...] = _log_forget(_dot(h, wf_ref[...]), lbl_ref[...], row)
    i_ref[...] = _dot(h, wi_ref[...]).astype(i_ref.dtype)
    sg_ref[...] = _silu(_dot(h, wg_ref[...])).astype(sg_ref.dtype)


def _inproj_b(x, mod3, g, w, lb_logits, row, seq):
    M, D = x.shape
    R = lb_logits.shape[0]
    tm = _tile(seq, 1024, SUBLANES)
    tn = _tile(D, MXU_WIDTH, LANES)
    per_sec = D // tn
    per_b = seq // tm

    def wspec(sec):
        return pl.BlockSpec((None, D, tn), lambda i, j: (row, 0, sec * per_sec + j))

    out = pl.BlockSpec((tm, tn), lambda i, j: (i, j))
    return pl.pallas_call(
        functools.partial(_inproj_b_kernel, row=row),
        grid=(M // tm, per_sec),
        in_specs=[
            pl.BlockSpec((tm, D), lambda i, j: (i, 0)),
            pl.BlockSpec((None, 3, D), lambda i, j: (i // per_b, 0, 0)),
            pl.BlockSpec((1, D), lambda i, j: (0, 0)),
            wspec(0), wspec(1), wspec(2), wspec(3),
            pl.BlockSpec((R, tn), lambda i, j: (0, j)),
        ],
        out_specs=[out, out, out, out],
        out_shape=[jax.ShapeDtypeStruct((M, D), BF16), jax.ShapeDtypeStruct((M, D), F32),
                   jax.ShapeDtypeStruct((M, D), BF16), jax.ShapeDtypeStruct((M, D), BF16)],
        scratch_shapes=[pltpu.VMEM((tm, D), BF16)],
        compiler_params=_params("parallel", "arbitrary"),
        name="inproj_b",
    )(x, mod3, g.reshape(1, D), w, w, w, w, lb_logits)


def _sgu_kernel(u_ref, v_ref, lng_ref, lnb_ref, ws_ref, bs_ref, o_ref, *, groups):
    tm, W = u_ref.shape
    gd = W // groups
    t_id = lax.broadcasted_iota(jnp.int32, (A_BLOCK, A_BLOCK), 0) // CHUNK
    s_id = lax.broadcasted_iota(jnp.int32, (A_BLOCK, A_BLOCK), 1) // CHUNK
    causal = s_id <= t_id
    lng = lng_ref[...]
    lnb = lnb_ref[...]
    for r in range(tm // A_BLOCK):
        rows = pl.ds(r * A_BLOCK, A_BLOCK)
        v = v_ref[rows, :].astype(F32)
        mu = jnp.mean(v, axis=-1, keepdims=True)
        vc = v - mu
        vn = vc * lax.rsqrt(jnp.mean(vc * vc, axis=-1, keepdims=True) + EPS) * lng + lnb
        vn = vn.astype(BF16)
        for g in range(groups):
            cols = pl.ds(g * gd, gd)
            w = jnp.where(causal, ws_ref[g], 0.0).astype(BF16)
            mixed = _dot(w, vn[:, g * gd:(g + 1) * gd]) + bs_ref[:, g:g + 1]
            o_ref[rows, cols] = (u_ref[rows, cols].astype(F32) * mixed).astype(o_ref.dtype)


def _sgu(z, ln_g, ln_b, w_s, b_s):
    M, W2 = z.shape
    W = W2 // 2
    G = w_s.shape[0]
    tm = 2 * A_BLOCK if M % (2 * A_BLOCK) == 0 else A_BLOCK
    return pl.pallas_call(
        functools.partial(_sgu_kernel, groups=G),
        grid=(M // tm,),
        in_specs=[
            pl.BlockSpec((tm, W), lambda i: (i, 0)),
            pl.BlockSpec((tm, W), lambda i: (i, 1)),
            pl.BlockSpec((1, W), lambda i: (0, 0)),
            pl.BlockSpec((1, W), lambda i: (0, 0)),
            pl.BlockSpec((G, A_BLOCK, A_BLOCK), lambda i: (0, 0, 0)),
            pl.BlockSpec((A_BLOCK, G), lambda i: (0, 0)),
        ],
        out_specs=pl.BlockSpec((tm, W), lambda i: (i, 0)),
        out_shape=jax.ShapeDtypeStruct((M, W), BF16),
        compiler_params=_params("parallel"),
        name="sgu",
    )(z, z, ln_g.reshape(1, W), ln_b.reshape(1, W), w_s, b_s.T)


GROUP_CHUNKS = MXU_WIDTH // CHUNK


def _level_sizes():
    sizes, n = [], CHUNK // 2
    while n >= 1:
        sizes.append(n)
        n //= 2
    return sizes


def _decay_tables():
    L = CHUNK
    sizes = _level_sizes()
    D = np.zeros((2 + len(sizes), L, L), np.float32)
    lid = np.full((L, L), -1, np.int32)
    r = np.arange(L)
    for t in range(L):
        D[0, t] = r <= t
        D[1, t] = r > t
        lid[t, t] = 0
    for li, n in enumerate(sizes):
        for t in range(L):
            m = (t // (2 * n)) * 2 * n + n - 1
            if (t // n) % 2 == 1:
                D[2 + li, t] = (r > m) & (r <= t)
                lid[t, m - n + 1:m + 1] = li + 1
            else:
                D[2 + li, t] = (r > t) & (r <= m)
    D = D.reshape(-1, L)
    return np.concatenate([D, D, D], axis=1), lid


def _split3(x):
    hi = x.astype(BF16)
    r1 = x - hi.astype(F32)
    mid = r1.astype(BF16)
    lo = (r1 - mid.astype(F32)).astype(BF16)
    return jnp.concatenate([hi, mid, lo], axis=0)


def _recur_kernel(q_ref, lf_ref, v_ref, g_ref, ng_ref, dcat_ref, lid_ref, o_ref, s_ref):
    @pl.when(pl.program_id(2) == 0)
    def _():
        s_ref[...] = jnp.zeros_like(s_ref)

    ts, dk = q_ref.shape
    n_chunks = ts // CHUNK
    n_lvl = len(_level_sizes())
    gr = lid_ref.shape[0]
    lid = lid_ref[...]
    ng = ng_ref[...]

    lf_wide = jnp.concatenate([lf_ref[pl.ds(c * CHUNK, CHUNK), :] for c in range(n_chunks)], axis=1)
    e = jnp.exp(_dot(dcat_ref[...], _split3(lf_wide)))

    def decay(b):
        return jnp.concatenate(
            [e[b * CHUNK:(b + 1) * CHUNK, c * dk:(c + 1) * dk] for c in range(n_chunks)], axis=0)

    qb = q_ref[...]
    q = qb.astype(F32)
    k = 1.0 - jnp.exp(lf_ref[...])
    vb = v_ref[...]
    qs = (q * decay(0)).astype(BF16)
    ks = (k * decay(1)).astype(BF16)
    ql = [qb] + [(q * decay(2 + li)).astype(BF16) for li in range(n_lvl)]
    kl = [k.astype(BF16)] + [(k * decay(2 + li)).astype(BF16) for li in range(n_lvl)]

    intra = []
    for g0 in range(0, ts, gr):
        rs = slice(g0, g0 + gr)
        sc = jnp.where(lid == 0, _dot_nt(ql[0][rs], kl[0][rs]), 0.0)
        for li in range(1, n_lvl + 1):
            sc = jnp.where(lid == li, _dot_nt(ql[li][rs], kl[li][rs]), sc)
        intra.append(_dot(sc.astype(BF16), vb[rs]))

    upd = [_dot_tn(vb[r0:r0 + CHUNK], ks[r0:r0 + CHUNK]) for r0 in range(0, ts, CHUNK)]
    states = [s_ref[...]]
    for c in range(n_chunks):
        e_last = e[CHUNK - 1:CHUNK, c * dk:(c + 1) * dk]
        states.append(states[c] * e_last + upd[c])
    s_ref[...] = states[n_chunks]
    for c in range(n_chunks):
        r0 = c * CHUNK
        rows = pl.ds(r0, CHUNK)
        o = _dot_nt(qs[r0:r0 + CHUNK], states[c].astype(BF16)) + intra[r0 // gr][r0 % gr:r0 % gr + CHUNK]
        o_ref[rows, :] = (_rms(o) * ng * g_ref[rows, :].astype(F32)).astype(o_ref.dtype)


def _recurrence(q, lf, iv, sg, norm_g, batch, seq):
    M, D = q.shape
    H, dv = norm_g.shape
    ts = _tile(seq, 1024, CHUNK)
    nt = seq // ts
    group = min(GROUP_CHUNKS, ts // CHUNK)
    assert (ts // CHUNK) % group == 0
    dcat, lid = _decay_tables()
    lid_g = np.full((group * CHUNK, group * CHUNK), -1, np.int32)
    for c in range(group):
        lid_g[c * CHUNK:(c + 1) * CHUNK, c * CHUNK:(c + 1) * CHUNK] = lid

    head = pl.BlockSpec((ts, dv), lambda b, h, t: (b * nt + t, h))
    return pl.pallas_call(
        _recur_kernel,
        grid=(batch, H, nt),
        in_specs=[
            head, head, head, head,
            pl.BlockSpec((None, 1, dv), lambda b, h, t: (h, 0, 0)),
            pl.BlockSpec(dcat.shape, lambda b, h, t: (0, 0)),
            pl.BlockSpec(lid_g.shape, lambda b, h, t: (0, 0)),
        ],
        out_specs=head,
        out_shape=jax.ShapeDtypeStruct((M, D), BF16),
        scratch_shapes=[pltpu.VMEM((dv, dv), F32)],
        compiler_params=_params("parallel", "parallel", "arbitrary"),
        name="recurrence",
    )(q, lf, iv, sg, norm_g.reshape(H, 1, dv), jnp.asarray(dcat, BF16), jnp.asarray(lid_g))


def _outproj_kernel(p_ref, w_ref, x_ref, mod_ref, o_ref):
    o_ref[...] = x_ref[...] + mod_ref[2:3, :] * _dot(p_ref[...], w_ref[...])


def _outproj_tiles(seq, K, D):
    budget = VMEM_LIMIT_BYTES * 3 // 4
    best = None
    for tm in (1024, 512, 256, 128):
        for tn in (D, 1024, 512, 256, 128):
            if seq % tm or D % tn or tn > D:
                continue
            need = 2 * (tm * K * 2 + K * tn * 2 + 2 * tm * tn * 4) + tm * tn * 4
            if need <= budget and (best is None or tm * tn > best[0] * best[1]):
                best = (tm, tn)
    assert best is not None
    return best


def _outproj(p, w, idx, x, mod3, seq):
    M, K = p.shape
    D = w.shape[-1]
    tm, tn = _outproj_tiles(seq, K, D)
    per_b = seq // tm
    return pl.pallas_call(
        _outproj_kernel,
        grid=(M // tm, D // tn),
        in_specs=[
            pl.BlockSpec((tm, K), lambda i, j: (i, 0)),
            pl.BlockSpec((None, K, tn), lambda i, j: (idx, 0, j)),
            pl.BlockSpec((tm, tn), lambda i, j: (i, j)),
            pl.BlockSpec((None, 3, tn), lambda i, j: (i // per_b, 0, j)),
        ],
        out_specs=pl.BlockSpec((tm, tn), lambda i, j: (i, j)),
        out_shape=jax.ShapeDtypeStruct((M, D), F32),
        compiler_params=_params("parallel", "parallel"),
        name="outproj",
    )(p, w, x, mod3)


def kernel(x, c, mod_w, mod_b, norm_g, ffn_w1, ffn_w3, ffn_w2, a_w_in, a_ln_g, a_ln_b, a_w_s, a_b_s, a_w_out,
           b_w_in, b_lb_logits, b_norm_g, b_w_out, final_g):
    B, S, D = x.shape
    depth = mod_w.shape[0]
    n_mixers = 2
    assert S % A_BLOCK == 0 and A_BLOCK % CHUNK == 0 and D % LANES == 0
    w1, w3, w2 = ffn_w1.astype(BF16), ffn_w3.astype(BF16), ffn_w2.astype(BF16)
    a_in, a_out = a_w_in.astype(BF16), a_w_out.astype(BF16)
    b_in, b_out = b_w_in.astype(BF16), b_w_out.astype(BF16)

    mod = _mod_all(c, mod_w, mod_b)[:, :B].reshape(depth, B, N_SUB, 3, D)
    xf = x.reshape(B * S, D)
    for layer in range(depth):
        j = layer // n_mixers
        last = layer == depth - 1
        xf = _ffn(xf, mod[layer, :, 0], norm_g[layer, 0], w1, w3, w2, layer, 0, final_g, S, False)
        if layer % n_mixers == 0:
            z = _inproj_a(xf, mod[layer, :, 1], norm_g[layer, 1], a_in, j, S)
            p = _sgu(z, a_ln_g[j], a_ln_b[j], a_w_s[j], a_b_s[j])
            xf = _outproj(p, a_out, j, xf, mod[layer, :, 1], S)
        else:
            q, lf, iv, sg = _inproj_b(xf, mod[layer, :, 1], norm_g[layer, 1], b_in, b_lb_logits, j, S)
            p = _recurrence(q, lf, iv, sg, b_norm_g[j], B, S)
            xf = _outproj(p, b_out, j, xf, mod[layer, :, 1], S)
        xf = _ffn(xf, mod[layer, :, 2], norm_g[layer, 2], w1, w3, w2, layer, 1, final_g, S, last)
    return xf.reshape(B, S, D)
```

```python
import functools

import numpy as np
import jax
import jax.numpy as jnp
from jax import lax
from jax.experimental import pallas as pl
from jax.experimental.pallas import tpu as pltpu

EPS = 1e-6
CHUNK = 64
A_BLOCK = 128
N_SUB = 3
LANES = 128
SUBLANES = 8
MXU_WIDTH = 256
ROW_SLAB = 16
VMEM_LIMIT_BYTES = 60 * 1024 * 1024

F32 = jnp.float32
BF16 = jnp.bfloat16


def _tile(dim, target, align):
    if dim <= target:
        return dim
    t = (target // align) * align
    while t >= align:
        if dim % t == 0:
            return t
        t -= align
    return dim


def _params(*semantics):
    return pltpu.CompilerParams(dimension_semantics=semantics, vmem_limit_bytes=VMEM_LIMIT_BYTES)


def _dot(a, b):
    return jnp.dot(a, b, preferred_element_type=F32)


def _dot_nt(a, b):
    return lax.dot_general(a, b, (((1,), (1,)), ((), ())), preferred_element_type=F32)


def _dot_tn(a, b):
    return lax.dot_general(a, b, (((0,), (0,)), ((), ())), preferred_element_type=F32)


def _silu(x):
    return x * jax.nn.sigmoid(x)


def _rms(x):
    return x * lax.rsqrt(jnp.mean(x * x, axis=-1, keepdims=True) + EPS)


def _norm_mod_rows(x_ref, mod_ref, g_ref, h_ref, copy_ref=None):
    gain = g_ref[...] * (1.0 + mod_ref[1:2, :])
    shift = mod_ref[0:1, :]

    def rows_body(r, carry):
        rows = pl.ds(pl.multiple_of(r * ROW_SLAB, ROW_SLAB), ROW_SLAB)
        x = x_ref[rows, :]
        h_ref[rows, :] = (_rms(x) * gain + shift).astype(h_ref.dtype)
        if copy_ref is not None:
            copy_ref[rows, :] = x
        return carry

    lax.fori_loop(0, x_ref.shape[0] // ROW_SLAB, rows_body, 0, unroll=4)


def _mod_kernel(c_ref, w_ref, b_ref, o_ref):
    o_ref[...] = _dot(_silu(c_ref[...]), w_ref[...]) + b_ref[...]


def _mod_all(c, mod_w, mod_b):
    L, D, N = mod_w.shape
    B = c.shape[0]
    Bp = -(-B // SUBLANES) * SUBLANES
    c_pad = jnp.zeros((Bp, D), F32).at[:B].set(c)
    tn = _tile(N, 1024, LANES)
    return pl.pallas_call(
        _mod_kernel,
        grid=(L, N // tn),
        in_specs=[
            pl.BlockSpec((Bp, D), lambda l, j: (0, 0)),
            pl.BlockSpec((None, D, tn), lambda l, j: (l, 0, j)),
            pl.BlockSpec((None, 1, tn), lambda l, j: (l, 0, j)),
        ],
        out_specs=pl.BlockSpec((None, Bp, tn), lambda l, j: (l, 0, j)),
        out_shape=jax.ShapeDtypeStruct((L, Bp, N), F32),
        compiler_params=_params("parallel", "parallel"),
        name="mod",
    )(c_pad, mod_w, mod_b.reshape(L, 1, N))


def _ffn_kernel(x_ref, mod_ref, g_ref, w1_ref, w3_ref, w2_ref, fg_ref, o_ref, h_ref, *, final_norm):
    j = pl.program_id(1)

    @pl.when(j == 0)
    def _():
        _norm_mod_rows(x_ref, mod_ref, g_ref, h_ref, copy_ref=o_ref)

    h = h_ref[...]
    a = _dot(h, w1_ref[...])
    b = _dot(h, w3_ref[...])
    p = (_silu(a) * b).astype(BF16)
    D = o_ref.shape[1]
    slab = min(D, 2 * MXU_WIDTH)
    for n0 in range(0, D, slab):
        cols = pl.ds(n0, slab)
        o_ref[:, cols] += (0.5 * mod_ref[2:3, cols]) * _dot(p, w2_ref[:, cols])

    if final_norm:
        @pl.when(j == pl.num_programs(1) - 1)
        def _():
            o_ref[...] = _rms(o_ref[...]) * fg_ref[...]


def _ffn(x, mod3, g, w1, w3, w2, layer, half, final_g, seq, final_norm):
    M, D = x.shape
    Fd = w1.shape[-1]
    tm = _tile(seq, 1024, SUBLANES)
    tf = _tile(Fd, 512, LANES)
    per_b = seq // tm
    return pl.pallas_call(
        functools.partial(_ffn_kernel, final_norm=final_norm),
        grid=(M // tm, Fd // tf),
        in_specs=[
            pl.BlockSpec((tm, D), lambda i, j: (i, 0)),
            pl.BlockSpec((None, 3, D), lambda i, j: (i // per_b, 0, 0)),
            pl.BlockSpec((1, D), lambda i, j: (0, 0)),
            pl.BlockSpec((None, None, D, tf), lambda i, j: (layer, half, 0, j)),
            pl.BlockSpec((None, None, D, tf), lambda i, j: (layer, half, 0, j)),
            pl.BlockSpec((None, None, tf, D), lambda i, j: (layer, half, j, 0)),
            pl.BlockSpec((1, D), lambda i, j: (0, 0)),
        ],
        out_specs=pl.BlockSpec((tm, D), lambda i, j: (i, 0)),
        out_shape=jax.ShapeDtypeStruct((M, D), F32),
        scratch_shapes=[pltpu.VMEM((tm, D), BF16)],
        compiler_params=_params("parallel", "arbitrary"),
        name="ffn",
    )(x, mod3, g.reshape(1, D), w1, w3, w2, final_g.reshape(1, D))


def _gelu(x):
    return 0.5 * x * (1.0 + lax.erf(x * np.float32(1.0 / np.sqrt(2.0))))


def _inproj_a_kernel(x_ref, mod_ref, g_ref, w_ref, o_ref, h_ref):
    @pl.when(pl.program_id(1) == 0)
    def _():
        _norm_mod_rows(x_ref, mod_ref, g_ref, h_ref)

    o_ref[...] = _gelu(_dot(h_ref[...], w_ref[...])).astype(o_ref.dtype)


def _inproj_a(x, mod3, g, w, idx, seq):
    M, D = x.shape
    N = w.shape[-1]
    tm = _tile(seq, 1024, SUBLANES)
    tn = _tile(N, 1024, LANES)
    per_b = seq // tm
    return pl.pallas_call(
        _inproj_a_kernel,
        grid=(M // tm, N // tn),
        in_specs=[
            pl.BlockSpec((tm, D), lambda i, j: (i, 0)),
            pl.BlockSpec((None, 3, D), lambda i, j: (i // per_b, 0, 0)),
            pl.BlockSpec((1, D), lambda i, j: (0, 0)),
            pl.BlockSpec((None, D, tn), lambda i, j: (idx, 0, j)),
        ],
        out_specs=pl.BlockSpec((tm, tn), lambda i, j: (i, j)),
        out_shape=jax.ShapeDtypeStruct((M, N), BF16),
        scratch_shapes=[pltpu.VMEM((tm, D), BF16)],
        compiler_params=_params("parallel", "arbitrary"),
        name="inproj_a",
    )(x, mod3, g.reshape(1, D), w)


def _log_forget(x, lbl, row):
    m = jnp.max(lbl, axis=0, keepdims=True)
    e = jnp.exp(lbl - m)
    sm = e / jnp.sum(e, axis=0, keepdims=True)
    cum = sm[0:1]
    for r in range(1, row + 1):
        cum = cum + sm[r:r + 1]
    lb = jnp.maximum(cum - sm[0:1], 0.0)
    log_sig = jnp.minimum(x, 0.0) - jnp.log1p(jnp.exp(-jnp.abs(x)))
    a = jnp.log(lb)
    b = jnp.log1p(-lb) + log_sig
    hi = jnp.maximum(a, b)
    lo = jnp.minimum(a, b)
    return hi + jnp.log1p(jnp.exp(lo - hi))


B_TILE = MXU_WIDTH


def _interleave_sections(w, tn):
    R, D, N = w.shape
    per_sec = N // 4 // tn
    return w.reshape(R, D, 4, per_sec, tn).transpose(0, 1, 3, 2, 4).reshape(R, D, N)


def _inproj_b_kernel(x_ref, mod_ref, g_ref, w_ref, lbl_ref, q_ref, lf_ref, i_ref, sg_ref, h_ref, *, row):
    @pl.when(pl.program_id(1) == 0)
    def _():
        _norm_mod_rows(x_ref, mod_ref, g_ref, h_ref)

    tn = q_ref.shape[1]
    y = _dot(h_ref[...], w_ref[...])
    q_ref[...] = _silu(y[:, 0:tn]).astype(q_ref.dtype)
    lf_ref[...] = _log_forget(y[:, tn:2 * tn], lbl_ref[...], row)
    i_ref[...] = y[:, 2 * tn:3 * tn].astype(i_ref.dtype)
    sg_ref[...] = _silu(y[:, 3 * tn:4 * tn]).astype(sg_ref.dtype)


def _inproj_b(x, mod3, g, w, lb_logits, row, seq):
    M, D = x.shape
    R = lb_logits.shape[0]
    tm = _tile(seq, 1024, SUBLANES)
    tn = _tile(D, B_TILE, LANES)
    per_b = seq // tm
    out = pl.BlockSpec((tm, tn), lambda i, j: (i, j))
    return pl.pallas_call(
        functools.partial(_inproj_b_kernel, row=row),
        grid=(M // tm, D // tn),
        in_specs=[
            pl.BlockSpec((tm, D), lambda i, j: (i, 0)),
            pl.BlockSpec((None, 3, D), lambda i, j: (i // per_b, 0, 0)),
            pl.BlockSpec((1, D), lambda i, j: (0, 0)),
            pl.BlockSpec((None, D, 4 * tn), lambda i, j: (row, 0, j)),
            pl.BlockSpec((R, tn), lambda i, j: (0, j)),
        ],
        out_specs=[out, out, out, out],
        out_shape=[jax.ShapeDtypeStruct((M, D), BF16), jax.ShapeDtypeStruct((M, D), F32),
                   jax.ShapeDtypeStruct((M, D), BF16), jax.ShapeDtypeStruct((M, D), BF16)],
        scratch_shapes=[pltpu.VMEM((tm, D), BF16)],
        compiler_params=_params("parallel", "arbitrary"),
        name="inproj_b",
    )(x, mod3, g.reshape(1, D), w, lb_logits)


def _sgu_kernel(u_ref, v_ref, lng_ref, lnb_ref, ws_ref, bs_ref, o_ref, *, groups):
    tm, W = u_ref.shape
    gd = W // groups
    t_id = lax.broadcasted_iota(jnp.int32, (A_BLOCK, A_BLOCK), 0) // CHUNK
    s_id = lax.broadcasted_iota(jnp.int32, (A_BLOCK, A_BLOCK), 1) // CHUNK
    causal = s_id <= t_id
    lng = lng_ref[...]
    lnb = lnb_ref[...]
    for r in range(tm // A_BLOCK):
        rows = pl.ds(r * A_BLOCK, A_BLOCK)
        v = v_ref[rows, :].astype(F32)
        mu = jnp.mean(v, axis=-1, keepdims=True)
        vc = v - mu
        vn = vc * lax.rsqrt(jnp.mean(vc * vc, axis=-1, keepdims=True) + EPS) * lng + lnb
        vn = vn.astype(BF16)
        for g in range(groups):
            cols = pl.ds(g * gd, gd)
            w = jnp.where(causal, ws_ref[g], 0.0).astype(BF16)
            mixed = _dot(w, vn[:, g * gd:(g + 1) * gd]) + bs_ref[:, g:g + 1]
            o_ref[rows, cols] = (u_ref[rows, cols].astype(F32) * mixed).astype(o_ref.dtype)


def _sgu(z, ln_g, ln_b, w_s, b_s):
    M, W2 = z.shape
    W = W2 // 2
    G = w_s.shape[0]
    tm = 2 * A_BLOCK if M % (2 * A_BLOCK) == 0 else A_BLOCK
    return pl.pallas_call(
        functools.partial(_sgu_kernel, groups=G),
        grid=(M // tm,),
        in_specs=[
            pl.BlockSpec((tm, W), lambda i: (i, 0)),
            pl.BlockSpec((tm, W), lambda i: (i, 1)),
            pl.BlockSpec((1, W), lambda i: (0, 0)),
            pl.BlockSpec((1, W), lambda i: (0, 0)),
            pl.BlockSpec((G, A_BLOCK, A_BLOCK), lambda i: (0, 0, 0)),
            pl.BlockSpec((A_BLOCK, G), lambda i: (0, 0)),
        ],
        out_specs=pl.BlockSpec((tm, W), lambda i: (i, 0)),
        out_shape=jax.ShapeDtypeStruct((M, W), BF16),
        compiler_params=_params("parallel"),
        name="sgu",
    )(z, z, ln_g.reshape(1, W), ln_b.reshape(1, W), w_s, b_s.T)


GROUP_CHUNKS = MXU_WIDTH // CHUNK


def _level_sizes():
    sizes, n = [], CHUNK // 2
    while n >= 1:
        sizes.append(n)
        n //= 2
    return sizes


def _decay_tables():
    L = CHUNK
    sizes = _level_sizes()
    D = np.zeros((2 + len(sizes), L, L), np.float32)
    lid = np.full((L, L), -1, np.int32)
    r = np.arange(L)
    for t in range(L):
        D[0, t] = r <= t
        D[1, t] = r > t
        lid[t, t] = 0
    for li, n in enumerate(sizes):
        for t in range(L):
            m = (t // (2 * n)) * 2 * n + n - 1
            if (t // n) % 2 == 1:
                D[2 + li, t] = (r > m) & (r <= t)
                lid[t, m - n + 1:m + 1] = li + 1
            else:
                D[2 + li, t] = (r > t) & (r <= m)
    D = D.reshape(-1, L)
    return np.concatenate([D, D, D], axis=1), lid


def _split3(x):
    hi = x.astype(BF16)
    r1 = x - hi.astype(F32)
    mid = r1.astype(BF16)
    lo = (r1 - mid.astype(F32)).astype(BF16)
    return jnp.concatenate([hi, mid, lo], axis=0)


def _recur_kernel(q_ref, lf_ref, v_ref, g_ref, ng_ref, dcat_ref, lid_ref, o_ref, s_ref):
    @pl.when(pl.program_id(2) == 0)
    def _():
        s_ref[...] = jnp.zeros_like(s_ref)

    ts, dk = q_ref.shape
    n_chunks = ts // CHUNK
    n_lvl = len(_level_sizes())
    gr = lid_ref.shape[0]
    lid = lid_ref[...]
    ng = ng_ref[...]

    lf_wide = jnp.concatenate([lf_ref[pl.ds(c * CHUNK, CHUNK), :] for c in range(n_chunks)], axis=1)
    e = jnp.exp(_dot(dcat_ref[...], _split3(lf_wide)))

    def decay(b):
        return jnp.concatenate(
            [e[b * CHUNK:(b + 1) * CHUNK, c * dk:(c + 1) * dk] for c in range(n_chunks)], axis=0)

    qb = q_ref[...]
    q = qb.astype(F32)
    k = 1.0 - jnp.exp(lf_ref[...])
    vb = v_ref[...]
    qs = (q * decay(0)).astype(BF16)
    ks = (k * decay(1)).astype(BF16)
    ql = [qb] + [(q * decay(2 + li)).astype(BF16) for li in range(n_lvl)]
    kl = [k.astype(BF16)] + [(k * decay(2 + li)).astype(BF16) for li in range(n_lvl)]

    intra = []
    for g0 in range(0, ts, gr):
        rs = slice(g0, g0 + gr)
        sc = jnp.where(lid == 0, _dot_nt(ql[0][rs], kl[0][rs]), 0.0)
        for li in range(1, n_lvl + 1):
            sc = jnp.where(lid == li, _dot_nt(ql[li][rs], kl[li][rs]), sc)
        intra.append(_dot(sc.astype(BF16), vb[rs]))

    upd = [_dot_tn(vb[r0:r0 + CHUNK], ks[r0:r0 + CHUNK]) for r0 in range(0, ts, CHUNK)]
    states = [s_ref[...]]
    for c in range(n_chunks):
        e_last = e[CHUNK - 1:CHUNK, c * dk:(c + 1) * dk]
        states.append(states[c] * e_last + upd[c])
    s_ref[...] = states[n_chunks]
    for c in range(n_chunks):
        r0 = c * CHUNK
        rows = pl.ds(r0, CHUNK)
        o = _dot_nt(qs[r0:r0 + CHUNK], states[c].astype(BF16)) + intra[r0 // gr][r0 % gr:r0 % gr + CHUNK]
        o_ref[rows, :] = (_rms(o) * ng * g_ref[rows, :].astype(F32)).astype(o_ref.dtype)


def _recurrence(q, lf, iv, sg, norm_g, batch, seq):
    M, D = q.shape
    H, dv = norm_g.shape
    ts = _tile(seq, 1024, CHUNK)
    nt = seq // ts
    group = min(GROUP_CHUNKS, ts // CHUNK)
    assert (ts // CHUNK) % group == 0
    dcat, lid = _decay_tables()
    lid_g = np.full((group * CHUNK, group * CHUNK), -1, np.int32)
    for c in range(group):
        lid_g[c * CHUNK:(c + 1) * CHUNK, c * CHUNK:(c + 1) * CHUNK] = lid

    head = pl.BlockSpec((ts, dv), lambda b, h, t: (b * nt + t, h))
    return pl.pallas_call(
        _recur_kernel,
        grid=(batch, H, nt),
        in_specs=[
            head, head, head, head,
            pl.BlockSpec((None, 1, dv), lambda b, h, t: (h, 0, 0)),
            pl.BlockSpec(dcat.shape, lambda b, h, t: (0, 0)),
            pl.BlockSpec(lid_g.shape, lambda b, h, t: (0, 0)),
        ],
        out_specs=head,
        out_shape=jax.ShapeDtypeStruct((M, D), BF16),
        scratch_shapes=[pltpu.VMEM((dv, dv), F32)],
        compiler_params=_params("parallel", "parallel", "arbitrary"),
        name="recurrence",
    )(q, lf, iv, sg, norm_g.reshape(H, 1, dv), jnp.asarray(dcat, BF16), jnp.asarray(lid_g))


def _outproj_kernel(p_ref, w_ref, x_ref, mod_ref, o_ref):
    o_ref[...] = x_ref[...] + mod_ref[2:3, :] * _dot(p_ref[...], w_ref[...])


def _outproj_tiles(seq, K, D):
    budget = VMEM_LIMIT_BYTES * 3 // 4
    best = None
    for tm in (1024, 512, 256, 128):
        for tn in (D, 1024, 512, 256, 128):
            if seq % tm or D % tn or tn > D:
                continue
            need = 2 * (tm * K * 2 + K * tn * 2 + 2 * tm * tn * 4) + tm * tn * 4
            if need <= budget and (best is None or tm * tn > best[0] * best[1]):
                best = (tm, tn)
    assert best is not None
    return best


def _outproj(p, w, idx, x, mod3, seq):
    M, K = p.shape
    D = w.shape[-1]
    tm, tn = _outproj_tiles(seq, K, D)
    per_b = seq // tm
    return pl.pallas_call(
        _outproj_kernel,
        grid=(M // tm, D // tn),
        in_specs=[
            pl.BlockSpec((tm, K), lambda i, j: (i, 0)),
            pl.BlockSpec((None, K, tn), lambda i, j: (idx, 0, j)),
            pl.BlockSpec((tm, tn), lambda i, j: (i, j)),
            pl.BlockSpec((None, 3, tn), lambda i, j: (i // per_b, 0, j)),
        ],
        out_specs=pl.BlockSpec((tm, tn), lambda i, j: (i, j)),
        out_shape=jax.ShapeDtypeStruct((M, D), F32),
        compiler_params=_params("parallel", "parallel"),
        name="outproj",
    )(p, w, x, mod3)


def kernel(x, c, mod_w, mod_b, norm_g, ffn_w1, ffn_w3, ffn_w2, a_w_in, a_ln_g, a_ln_b, a_w_s, a_b_s, a_w_out,
           b_w_in, b_lb_logits, b_norm_g, b_w_out, final_g):
    B, S, D = x.shape
    depth = mod_w.shape[0]
    n_mixers = 2
    assert S % A_BLOCK == 0 and A_BLOCK % CHUNK == 0 and D % LANES == 0
    w1, w3, w2 = ffn_w1.astype(BF16), ffn_w3.astype(BF16), ffn_w2.astype(BF16)
    a_in, a_out = a_w_in.astype(BF16), a_w_out.astype(BF16)
    b_in = _interleave_sections(b_w_in.astype(BF16), _tile(D, B_TILE, LANES))
    b_out = b_w_out.astype(BF16)

    mod = _mod_all(c, mod_w, mod_b)[:, :B].reshape(depth, B, N_SUB, 3, D)
    xf = x.reshape(B * S, D)
    for layer in range(depth):
        j = layer // n_mixers
        last = layer == depth - 1
        xf = _ffn(xf, mod[layer, :, 0], norm_g[layer, 0], w1, w3, w2, layer, 0, final_g, S, False)
        if layer % n_mixers == 0:
            z = _inproj_a(xf, mod[layer, :, 1], norm_g[layer, 1], a_in, j, S)
            p = _sgu(z, a_ln_g[j], a_ln_b[j], a_w_s[j], a_b_s[j])
            xf = _outproj(p, a_out, j, xf, mod[layer, :, 1], S)
        else:
            q, lf, iv, sg = _inproj_b(xf, mod[layer, :, 1], norm_g[layer, 1], b_in, b_lb_logits, j, S)
            p = _recurrence(q, lf, iv, sg, b_norm_g[j], B, S)
            xf = _outproj(p, b_out, j, xf, mod[layer, :, 1], S)
        xf = _ffn(xf, mod[layer, :, 2], norm_g[layer, 2], w1, w3, w2, layer, 1, final_g, S, last)
    return xf.reshape(B, S, D)
```

```python
import functools

import numpy as np
import jax
import jax.numpy as jnp
from jax import lax
from jax.experimental import pallas as pl
from jax.experimental.pallas import tpu as pltpu

EPS = 1e-6
CHUNK = 64
A_BLOCK = 128
N_SUB = 3
LANES = 128
SUBLANES = 8
MXU_WIDTH = 256
ROW_SLAB = 16
VMEM_LIMIT_BYTES = 60 * 1024 * 1024

F32 = jnp.float32
BF16 = jnp.bfloat16


def _tile(dim, target, align):
    if dim <= target:
        return dim
    t = (target // align) * align
    while t >= align:
        if dim % t == 0:
            return t
        t -= align
    return dim


def _params(*semantics):
    return pltpu.CompilerParams(dimension_semantics=semantics, vmem_limit_bytes=VMEM_LIMIT_BYTES)


def _dot(a, b):
    return jnp.dot(a, b, preferred_element_type=F32)


def _dot_nt(a, b):
    return lax.dot_general(a, b, (((1,), (1,)), ((), ())), preferred_element_type=F32)


def _dot_tn(a, b):
    return lax.dot_general(a, b, (((0,), (0,)), ((), ())), preferred_element_type=F32)


def _silu(x):
    return x * jax.nn.sigmoid(x)


def _rms(x):
    return x * lax.rsqrt(jnp.mean(x * x, axis=-1, keepdims=True) + EPS)


def _norm_mod_rows(x_ref, mod_ref, g_ref, h_ref, copy_ref=None):
    gain = g_ref[...] * (1.0 + mod_ref[1:2, :])
    shift = mod_ref[0:1, :]

    def rows_body(r, carry):
        rows = pl.ds(pl.multiple_of(r * ROW_SLAB, ROW_SLAB), ROW_SLAB)
        x = x_ref[rows, :]
        h_ref[rows, :] = (_rms(x) * gain + shift).astype(h_ref.dtype)
        if copy_ref is not None:
            copy_ref[rows, :] = x
        return carry

    lax.fori_loop(0, x_ref.shape[0] // ROW_SLAB, rows_body, 0, unroll=4)


def _mod_kernel(c_ref, w_ref, b_ref, o_ref):
    o_ref[...] = _dot(_silu(c_ref[...]), w_ref[...]) + b_ref[...]


def _mod_all(c, mod_w, mod_b):
    L, D, N = mod_w.shape
    B = c.shape[0]
    Bp = -(-B // SUBLANES) * SUBLANES
    c_pad = jnp.zeros((Bp, D), F32).at[:B].set(c)
    tn = _tile(N, 1024, LANES)
    return pl.pallas_call(
        _mod_kernel,
        grid=(L, N // tn),
        in_specs=[
            pl.BlockSpec((Bp, D), lambda l, j: (0, 0)),
            pl.BlockSpec((None, D, tn), lambda l, j: (l, 0, j)),
            pl.BlockSpec((None, 1, tn), lambda l, j: (l, 0, j)),
        ],
        out_specs=pl.BlockSpec((None, Bp, tn), lambda l, j: (l, 0, j)),
        out_shape=jax.ShapeDtypeStruct((L, Bp, N), F32),
        compiler_params=_params("parallel", "parallel"),
        name="mod",
    )(c_pad, mod_w, mod_b.reshape(L, 1, N))


def _ffn_kernel(x_ref, mod_ref, g_ref, w1_ref, w3_ref, w2_ref, fg_ref, o_ref, h_ref, *, final_norm):
    j = pl.program_id(1)

    @pl.when(j == 0)
    def _():
        _norm_mod_rows(x_ref, mod_ref, g_ref, h_ref, copy_ref=o_ref)

    h = h_ref[...]
    a = _dot(h, w1_ref[...])
    b = _dot(h, w3_ref[...])
    p = (_silu(a) * b).astype(BF16)
    D = o_ref.shape[1]
    slab = min(D, 2 * MXU_WIDTH)
    for n0 in range(0, D, slab):
        cols = pl.ds(n0, slab)
        o_ref[:, cols] += (0.5 * mod_ref[2:3, cols]) * _dot(p, w2_ref[:, cols])

    if final_norm:
        @pl.when(j == pl.num_programs(1) - 1)
        def _():
            o_ref[...] = _rms(o_ref[...]) * fg_ref[...]


def _ffn(x, mod3, g, w1, w3, w2, layer, half, final_g, seq, final_norm):
    M, D = x.shape
    Fd = w1.shape[-1]
    tm = _tile(seq, 1024, SUBLANES)
    tf = _tile(Fd, 512, LANES)
    per_b = seq // tm
    return pl.pallas_call(
        functools.partial(_ffn_kernel, final_norm=final_norm),
        grid=(M // tm, Fd // tf),
        in_specs=[
            pl.BlockSpec((tm, D), lambda i, j: (i, 0)),
            pl.BlockSpec((None, 3, D), lambda i, j: (i // per_b, 0, 0)),
            pl.BlockSpec((1, D), lambda i, j: (0, 0)),
            pl.BlockSpec((None, None, D, tf), lambda i, j: (layer, half, 0, j)),
            pl.BlockSpec((None, None, D, tf), lambda i, j: (layer, half, 0, j)),
            pl.BlockSpec((None, None, tf, D), lambda i, j: (layer, half, j, 0)),
            pl.BlockSpec((1, D), lambda i, j: (0, 0)),
        ],
        out_specs=pl.BlockSpec((tm, D), lambda i, j: (i, 0)),
        out_shape=jax.ShapeDtypeStruct((M, D), F32),
        scratch_shapes=[pltpu.VMEM((tm, D), BF16)],
        compiler_params=_params("parallel", "arbitrary"),
        name="ffn",
    )(x, mod3, g.reshape(1, D), w1, w3, w2, final_g.reshape(1, D))


def _gelu(x):
    return 0.5 * x * (1.0 + lax.erf(x * np.float32(1.0 / np.sqrt(2.0))))


def _inproj_a_kernel(x_ref, mod_ref, g_ref, w_ref, o_ref, h_ref):
    @pl.when(pl.program_id(1) == 0)
    def _():
        _norm_mod_rows(x_ref, mod_ref, g_ref, h_ref)

    o_ref[...] = _gelu(_dot(h_ref[...], w_ref[...])).astype(o_ref.dtype)


def _inproj_a(x, mod3, g, w, idx, seq):
    M, D = x.shape
    N = w.shape[-1]
    tm = _tile(seq, 1024, SUBLANES)
    tn = _tile(N, 1024, LANES)
    per_b = seq // tm
    return pl.pallas_call(
        _inproj_a_kernel,
        grid=(M // tm, N // tn),
        in_specs=[
            pl.BlockSpec((tm, D), lambda i, j: (i, 0)),
            pl.BlockSpec((None, 3, D), lambda i, j: (i // per_b, 0, 0)),
            pl.BlockSpec((1, D), lambda i, j: (0, 0)),
            pl.BlockSpec((None, D, tn), lambda i, j: (idx, 0, j)),
        ],
        out_specs=pl.BlockSpec((tm, tn), lambda i, j: (i, j)),
        out_shape=jax.ShapeDtypeStruct((M, N), BF16),
        scratch_shapes=[pltpu.VMEM((tm, D), BF16)],
        compiler_params=_params("parallel", "arbitrary"),
        name="inproj_a",
    )(x, mod3, g.reshape(1, D), w)


def _log_forget(x, lbl, row):
    m = jnp.max(lbl, axis=0, keepdims=True)
    e = jnp.exp(lbl - m)
    sm = e / jnp.sum(e, axis=0, keepdims=True)
    cum = sm[0:1]
    for r in range(1, row + 1):
        cum = cum + sm[r:r + 1]
    lb = jnp.maximum(cum - sm[0:1], 0.0)
    t = jnp.exp(-jnp.abs(x))
    pos = x >= 0.0
    num = jnp.log(jnp.where(pos, 1.0 + lb * t, t + lb))
    num = jnp.where(pos, num, jnp.maximum(num, x))
    return num - jnp.log(1.0 + t)


def _inproj_b_kernel(x_ref, mod_ref, g_ref, wq_ref, wf_ref, wi_ref, wg_ref, lbl_ref,
                     q_ref, lf_ref, i_ref, sg_ref, h_ref, *, row):
    @pl.when(pl.program_id(1) == 0)
    def _():
        _norm_mod_rows(x_ref, mod_ref, g_ref, h_ref)

    h = h_ref[...]
    q_ref[...] = _silu(_dot(h, wq_ref[...])).astype(q_ref.dtype)
    lf_ref[...] = _log_forget(_dot(h, wf_ref[...]), lbl_ref[...], row)
    i_ref[...] = _dot(h, wi_ref[...]).astype(i_ref.dtype)
    sg_ref[...] = _silu(_dot(h, wg_ref[...])).astype(sg_ref.dtype)


def _inproj_b(x, mod3, g, w, lb_logits, row, seq):
    M, D = x.shape
    R = lb_logits.shape[0]
    tm = _tile(seq, 1024, SUBLANES)
    tn = _tile(D, MXU_WIDTH, LANES)
    per_sec = D // tn
    per_b = seq // tm

    def wspec(sec):
        return pl.BlockSpec((None, D, tn), lambda i, j: (row, 0, sec * per_sec + j))

    out = pl.BlockSpec((tm, tn), lambda i, j: (i, j))
    return pl.pallas_call(
        functools.partial(_inproj_b_kernel, row=row),
        grid=(M // tm, per_sec),
        in_specs=[
            pl.BlockSpec((tm, D), lambda i, j: (i, 0)),
            pl.BlockSpec((None, 3, D), lambda i, j: (i // per_b, 0, 0)),
            pl.BlockSpec((1, D), lambda i, j: (0, 0)),
            wspec(0), wspec(1), wspec(2), wspec(3),
            pl.BlockSpec((R, tn), lambda i, j: (0, j)),
        ],
        out_specs=[out, out, out, out],
        out_shape=[jax.ShapeDtypeStruct((M, D), BF16), jax.ShapeDtypeStruct((M, D), F32),
                   jax.ShapeDtypeStruct((M, D), BF16), jax.ShapeDtypeStruct((M, D), BF16)],
        scratch_shapes=[pltpu.VMEM((tm, D), BF16)],
        compiler_params=_params("parallel", "arbitrary"),
        name="inproj_b",
    )(x, mod3, g.reshape(1, D), w, w, w, w, lb_logits)


def _sgu_kernel(u_ref, v_ref, lng_ref, lnb_ref, ws_ref, bs_ref, o_ref, *, groups):
    tm, W = u_ref.shape
    gd = W // groups
    t_id = lax.broadcasted_iota(jnp.int32, (A_BLOCK, A_BLOCK), 0) // CHUNK
    s_id = lax.broadcasted_iota(jnp.int32, (A_BLOCK, A_BLOCK), 1) // CHUNK
    causal = s_id <= t_id
    lng = lng_ref[...]
    lnb = lnb_ref[...]
    for r in range(tm // A_BLOCK):
        rows = pl.ds(r * A_BLOCK, A_BLOCK)
        v = v_ref[rows, :].astype(F32)
        mu = jnp.mean(v, axis=-1, keepdims=True)
        vc = v - mu
        vn = vc * lax.rsqrt(jnp.mean(vc * vc, axis=-1, keepdims=True) + EPS) * lng + lnb
        vn = vn.astype(BF16)
        for g in range(groups):
            cols = pl.ds(g * gd, gd)
            w = jnp.where(causal, ws_ref[g], 0.0).astype(BF16)
            mixed = _dot(w, vn[:, g * gd:(g + 1) * gd]) + bs_ref[:, g:g + 1]
            o_ref[rows, cols] = (u_ref[rows, cols].astype(F32) * mixed).astype(o_ref.dtype)


def _sgu(z, ln_g, ln_b, w_s, b_s):
    M, W2 = z.shape
    W = W2 // 2
    G = w_s.shape[0]
    tm = 2 * A_BLOCK if M % (2 * A_BLOCK) == 0 else A_BLOCK
    return pl.pallas_call(
        functools.partial(_sgu_kernel, groups=G),
        grid=(M // tm,),
        in_specs=[
            pl.BlockSpec((tm, W), lambda i: (i, 0)),
            pl.BlockSpec((tm, W), lambda i: (i, 1)),
            pl.BlockSpec((1, W), lambda i: (0, 0)),
            pl.BlockSpec((1, W), lambda i: (0, 0)),
            pl.BlockSpec((G, A_BLOCK, A_BLOCK), lambda i: (0, 0, 0)),
            pl.BlockSpec((A_BLOCK, G), lambda i: (0, 0)),
        ],
        out_specs=pl.BlockSpec((tm, W), lambda i: (i, 0)),
        out_shape=jax.ShapeDtypeStruct((M, W), BF16),
        compiler_params=_params("parallel"),
        name="sgu",
    )(z, z, ln_g.reshape(1, W), ln_b.reshape(1, W), w_s, b_s.T)


GROUP_CHUNKS = MXU_WIDTH // CHUNK


def _level_sizes():
    sizes, n = [], CHUNK // 2
    while n >= 1:
        sizes.append(n)
        n //= 2
    return sizes


def _decay_tables():
    L = CHUNK
    sizes = _level_sizes()
    D = np.zeros((2 + len(sizes), L, L), np.float32)
    lid = np.full((L, L), -1, np.int32)
    r = np.arange(L)
    for t in range(L):
        D[0, t] = r <= t
        D[1, t] = r > t
        lid[t, t] = 0
    for li, n in enumerate(sizes):
        for t in range(L):
            m = (t // (2 * n)) * 2 * n + n - 1
            if (t // n) % 2 == 1:
                D[2 + li, t] = (r > m) & (r <= t)
                lid[t, m - n + 1:m + 1] = li + 1
            else:
                D[2 + li, t] = (r > t) & (r <= m)
    D = D.reshape(-1, L)
    return np.concatenate([D, D, D], axis=1), lid


def _split3(x):
    hi = x.astype(BF16)
    r1 = x - hi.astype(F32)
    mid = r1.astype(BF16)
    lo = (r1 - mid.astype(F32)).astype(BF16)
    return jnp.concatenate([hi, mid, lo], axis=0)


def _recur_kernel(q_ref, lf_ref, v_ref, g_ref, ng_ref, dcat_ref, lid_ref, o_ref, s_ref):
    @pl.when(pl.program_id(2) == 0)
    def _():
        s_ref[...] = jnp.zeros_like(s_ref)

    ts, dk = q_ref.shape
    n_chunks = ts // CHUNK
    n_lvl = len(_level_sizes())
    gr = lid_ref.shape[0]
    lid = lid_ref[...]
    ng = ng_ref[...]

    lf_wide = jnp.concatenate([lf_ref[pl.ds(c * CHUNK, CHUNK), :] for c in range(n_chunks)], axis=1)
    e = jnp.exp(_dot(dcat_ref[...], _split3(lf_wide)))

    def decay(b):
        return jnp.concatenate(
            [e[b * CHUNK:(b + 1) * CHUNK, c * dk:(c + 1) * dk] for c in range(n_chunks)], axis=0)

    qb = q_ref[...]
    q = qb.astype(F32)
    k = 1.0 - jnp.exp(lf_ref[...])
    vb = v_ref[...]
    qs = (q * decay(0)).astype(BF16)
    ks = (k * decay(1)).astype(BF16)
    ql = [qb] + [(q * decay(2 + li)).astype(BF16) for li in range(n_lvl)]
    kl = [k.astype(BF16)] + [(k * decay(2 + li)).astype(BF16) for li in range(n_lvl)]

    intra = []
    for g0 in range(0, ts, gr):
        rs = slice(g0, g0 + gr)
        sc = jnp.where(lid == 0, _dot_nt(ql[0][rs], kl[0][rs]), 0.0)
        for li in range(1, n_lvl + 1):
            sc = jnp.where(lid == li, _dot_nt(ql[li][rs], kl[li][rs]), sc)
        intra.append(_dot(sc.astype(BF16), vb[rs]))

    upd = [_dot_tn(vb[r0:r0 + CHUNK], ks[r0:r0 + CHUNK]) for r0 in range(0, ts, CHUNK)]
    states = [s_ref[...]]
    for c in range(n_chunks):
        e_last = e[CHUNK - 1:CHUNK, c * dk:(c + 1) * dk]
        states.append(states[c] * e_last + upd[c])
    s_ref[...] = states[n_chunks]
    for c in range(n_chunks):
        r0 = c * CHUNK
        rows = pl.ds(r0, CHUNK)
        o = _dot_nt(qs[r0:r0 + CHUNK], states[c].astype(BF16)) + intra[r0 // gr][r0 % gr:r0 % gr + CHUNK]
        o_ref[rows, :] = (_rms(o) * ng * g_ref[rows, :].astype(F32)).astype(o_ref.dtype)


def _recurrence(q, lf, iv, sg, norm_g, batch, seq):
    M, D = q.shape
    H, dv = norm_g.shape
    ts = _tile(seq, 1024, CHUNK)
    nt = seq // ts
    group = min(GROUP_CHUNKS, ts // CHUNK)
    assert (ts // CHUNK) % group == 0
    dcat, lid = _decay_tables()
    lid_g = np.full((group * CHUNK, group * CHUNK), -1, np.int32)
    for c in range(group):
        lid_g[c * CHUNK:(c + 1) * CHUNK, c * CHUNK:(c + 1) * CHUNK] = lid

    head = pl.BlockSpec((ts, dv), lambda b, h, t: (b * nt + t, h))
    return pl.pallas_call(
        _recur_kernel,
        grid=(batch, H, nt),
        in_specs=[
            head, head, head, head,
            pl.BlockSpec((None, 1, dv), lambda b, h, t: (h, 0, 0)),
            pl.BlockSpec(dcat.shape, lambda b, h, t: (0, 0)),
            pl.BlockSpec(lid_g.shape, lambda b, h, t: (0, 0)),
        ],
        out_specs=head,
        out_shape=jax.ShapeDtypeStruct((M, D), BF16),
        scratch_shapes=[pltpu.VMEM((dv, dv), F32)],
        compiler_params=_params("parallel", "parallel", "arbitrary"),
        name="recurrence",
    )(q, lf, iv, sg, norm_g.reshape(H, 1, dv), jnp.asarray(dcat, BF16), jnp.asarray(lid_g))


def _outproj_kernel(p_ref, w_ref, x_ref, mod_ref, o_ref):
    o_ref[...] = x_ref[...] + mod_ref[2:3, :] * _dot(p_ref[...], w_ref[...])


def _outproj_tiles(seq, K, D):
    budget = VMEM_LIMIT_BYTES * 3 // 4
    best = None
    for tm in (1024, 512, 256, 128):
        for tn in (D, 1024, 512, 256, 128):
            if seq % tm or D % tn or tn > D:
                continue
            need = 2 * (tm * K * 2 + K * tn * 2 + 2 * tm * tn * 4) + tm * tn * 4
            if need <= budget and (best is None or tm * tn > best[0] * best[1]):
                best = (tm, tn)
    assert best is not None
    return best


def _outproj(p, w, idx, x, mod3, seq):
    M, K = p.shape
    D = w.shape[-1]
    tm, tn = _outproj_tiles(seq, K, D)
    per_b = seq // tm
    return pl.pallas_call(
        _outproj_kernel,
        grid=(M // tm, D // tn),
        in_specs=[
            pl.BlockSpec((tm, K), lambda i, j: (i, 0)),
            pl.BlockSpec((None, K, tn), lambda i, j: (idx, 0, j)),
            pl.BlockSpec((tm, tn), lambda i, j: (i, j)),
            pl.BlockSpec((None, 3, tn), lambda i, j: (i // per_b, 0, j)),
        ],
        out_specs=pl.BlockSpec((tm, tn), lambda i, j: (i, j)),
        out_shape=jax.ShapeDtypeStruct((M, D), F32),
        compiler_params=_params("parallel", "parallel"),
        name="outproj",
    )(p, w, x, mod3)


def kernel(x, c, mod_w, mod_b, norm_g, ffn_w1, ffn_w3, ffn_w2, a_w_in, a_ln_g, a_ln_b, a_w_s, a_b_s, a_w_out,
           b_w_in, b_lb_logits, b_norm_g, b_w_out, final_g):
    B, S, D = x.shape
    depth = mod_w.shape[0]
    n_mixers = 2
    assert S % A_BLOCK == 0 and A_BLOCK % CHUNK == 0 and D % LANES == 0
    w1, w3, w2 = ffn_w1.astype(BF16), ffn_w3.astype(BF16), ffn_w2.astype(BF16)
    a_in, a_out = a_w_in.astype(BF16), a_w_out.astype(BF16)
    b_in, b_out = b_w_in.astype(BF16), b_w_out.astype(BF16)

    mod = _mod_all(c, mod_w, mod_b)[:, :B].reshape(depth, B, N_SUB, 3, D)
    xf = x.reshape(B * S, D)
    for layer in range(depth):
        j = layer // n_mixers
        last = layer == depth - 1
        xf = _ffn(xf, mod[layer, :, 0], norm_g[layer, 0], w1, w3, w2, layer, 0, final_g, S, False)
        if layer % n_mixers == 0:
            z = _inproj_a(xf, mod[layer, :, 1], norm_g[layer, 1], a_in, j, S)
            p = _sgu(z, a_ln_g[j], a_ln_b[j], a_w_s[j], a_b_s[j])
            xf = _outproj(p, a_out, j, xf, mod[layer, :, 1], S)
        else:
            q, lf, iv, sg = _inproj_b(xf, mod[layer, :, 1], norm_g[layer, 1], b_in, b_lb_logits, j, S)
            p = _recurrence(q, lf, iv, sg, b_norm_g[j], B, S)
            xf = _outproj(p, b_out, j, xf, mod[layer, :, 1], S)
        xf = _ffn(xf, mod[layer, :, 2], norm_g[layer, 2], w1, w3, w2, layer, 1, final_g, S, last)
    return xf.reshape(B, S, D)
```

```python
import functools

import numpy as np
import jax
import jax.numpy as jnp
from jax import lax
from jax.experimental import pallas as pl
from jax.experimental.pallas import tpu as pltpu

EPS = 1e-6
CHUNK = 64
A_BLOCK = 128
N_SUB = 3
LANES = 128
SUBLANES = 8
MXU_WIDTH = 256
ROW_SLAB = 16
VMEM_LIMIT_BYTES = 60 * 1024 * 1024

F32 = jnp.float32
BF16 = jnp.bfloat16


def _tile(dim, target, align):
    if dim <= target:
        return dim
    t = (target // align) * align
    while t >= align:
        if dim % t == 0:
            return t
        t -= align
    return dim


def _params(*semantics, flags=None):
    return pltpu.CompilerParams(dimension_semantics=semantics, vmem_limit_bytes=VMEM_LIMIT_BYTES, flags=flags)


def _dot(a, b):
    return jnp.dot(a, b, preferred_element_type=F32)


def _dot_nt(a, b):
    return lax.dot_general(a, b, (((1,), (1,)), ((), ())), preferred_element_type=F32)


def _dot_tn(a, b):
    return lax.dot_general(a, b, (((0,), (0,)), ((), ())), preferred_element_type=F32)


def _silu(x):
    return x * jax.nn.sigmoid(x)


def _rms(x):
    return x * lax.rsqrt(jnp.mean(x * x, axis=-1, keepdims=True) + EPS)


def _norm_mod_rows(x_ref, mod_ref, g_ref, h_ref, copy_ref=None):
    gain = g_ref[...] * (1.0 + mod_ref[1:2, :])
    shift = mod_ref[0:1, :]

    def rows_body(r, carry):
        rows = pl.ds(pl.multiple_of(r * ROW_SLAB, ROW_SLAB), ROW_SLAB)
        x = x_ref[rows, :]
        h_ref[rows, :] = (_rms(x) * gain + shift).astype(h_ref.dtype)
        if copy_ref is not None:
            copy_ref[rows, :] = x
        return carry

    lax.fori_loop(0, x_ref.shape[0] // ROW_SLAB, rows_body, 0, unroll=4)


def _mod_kernel(c_ref, w_ref, b_ref, o_ref):
    o_ref[...] = _dot(_silu(c_ref[...]), w_ref[...]) + b_ref[...]


def _mod_all(c, mod_w, mod_b):
    L, D, N = mod_w.shape
    B = c.shape[0]
    Bp = -(-B // SUBLANES) * SUBLANES
    c_pad = jnp.zeros((Bp, D), F32).at[:B].set(c)
    tn = _tile(N, 1024, LANES)
    return pl.pallas_call(
        _mod_kernel,
        grid=(L, N // tn),
        in_specs=[
            pl.BlockSpec((Bp, D), lambda l, j: (0, 0)),
            pl.BlockSpec((None, D, tn), lambda l, j: (l, 0, j)),
            pl.BlockSpec((None, 1, tn), lambda l, j: (l, 0, j)),
        ],
        out_specs=pl.BlockSpec((None, Bp, tn), lambda l, j: (l, 0, j)),
        out_shape=jax.ShapeDtypeStruct((L, Bp, N), F32),
        compiler_params=_params("parallel", "parallel"),
        name="mod",
    )(c_pad, mod_w, mod_b.reshape(L, 1, N))


def _ffn_kernel(x_ref, mod_ref, g_ref, w1_ref, w3_ref, w2_ref, fg_ref, o_ref, h_ref, *, final_norm):
    j = pl.program_id(1)

    @pl.when(j == 0)
    def _():
        _norm_mod_rows(x_ref, mod_ref, g_ref, h_ref, copy_ref=o_ref)

    h = h_ref[...]
    a = _dot(h, w1_ref[...])
    b = _dot(h, w3_ref[...])
    p = (_silu(a) * b).astype(BF16)
    D = o_ref.shape[1]
    slab = min(D, 2 * MXU_WIDTH)
    for n0 in range(0, D, slab):
        cols = pl.ds(n0, slab)
        o_ref[:, cols] += (0.5 * mod_ref[2:3, cols]) * _dot(p, w2_ref[:, cols])

    if final_norm:
        @pl.when(j == pl.num_programs(1) - 1)
        def _():
            o_ref[...] = _rms(o_ref[...]) * fg_ref[...]


def _ffn(x, mod3, g, w1, w3, w2, layer, half, final_g, seq, final_norm):
    M, D = x.shape
    Fd = w1.shape[-1]
    tm = _tile(seq, 1024, SUBLANES)
    tf = _tile(Fd, 512, LANES)
    per_b = seq // tm
    return pl.pallas_call(
        functools.partial(_ffn_kernel, final_norm=final_norm),
        grid=(M // tm, Fd // tf),
        in_specs=[
            pl.BlockSpec((tm, D), lambda i, j: (i, 0)),
            pl.BlockSpec((None, 3, D), lambda i, j: (i // per_b, 0, 0)),
            pl.BlockSpec((1, D), lambda i, j: (0, 0)),
            pl.BlockSpec((None, None, D, tf), lambda i, j: (layer, half, 0, j)),
            pl.BlockSpec((None, None, D, tf), lambda i, j: (layer, half, 0, j)),
            pl.BlockSpec((None, None, tf, D), lambda i, j: (layer, half, j, 0)),
            pl.BlockSpec((1, D), lambda i, j: (0, 0)),
        ],
        out_specs=pl.BlockSpec((tm, D), lambda i, j: (i, 0)),
        out_shape=jax.ShapeDtypeStruct((M, D), F32),
        scratch_shapes=[pltpu.VMEM((tm, D), BF16)],
        compiler_params=_params("parallel", "arbitrary"),
        name="ffn",
    )(x, mod3, g.reshape(1, D), w1, w3, w2, final_g.reshape(1, D))


def _gelu(x):
    return 0.5 * x * (1.0 + lax.erf(x * np.float32(1.0 / np.sqrt(2.0))))


def _fold_lanes(a):
    out = a[:, 0:LANES]
    for c0 in range(LANES, a.shape[1], LANES):
        out = out + a[:, c0:c0 + LANES]
    return out


def _inproj_a_kernel(x_ref, mod_ref, g_ref, w_ref, o_ref, s1_ref, s2_ref, h_ref, *, first_v):
    j = pl.program_id(1)

    @pl.when(j == 0)
    def _():
        _norm_mod_rows(x_ref, mod_ref, g_ref, h_ref)
        s1_ref[...] = jnp.zeros_like(s1_ref)
        s2_ref[...] = jnp.zeros_like(s2_ref)

    y = _gelu(_dot(h_ref[...], w_ref[...]))
    o_ref[...] = y.astype(o_ref.dtype)
    is_v = (j >= first_v).astype(F32)
    s1_ref[...] += is_v * _fold_lanes(y)
    s2_ref[...] += is_v * _fold_lanes(y * y)


def _inproj_a(x, mod3, g, w, idx, seq):
    M, D = x.shape
    N = w.shape[-1]
    tm = _tile(seq, 1024, SUBLANES)
    tn = _tile(N // 2, 2048, LANES)
    per_b = seq // tm
    stat = pl.BlockSpec((tm, LANES), lambda i, j: (i, 0))
    return pl.pallas_call(
        functools.partial(_inproj_a_kernel, first_v=N // 2 // tn),
        grid=(M // tm, N // tn),
        in_specs=[
            pl.BlockSpec((tm, D), lambda i, j: (i, 0)),
            pl.BlockSpec((None, 3, D), lambda i, j: (i // per_b, 0, 0)),
            pl.BlockSpec((1, D), lambda i, j: (0, 0)),
            pl.BlockSpec((None, D, tn), lambda i, j: (idx, 0, j)),
        ],
        out_specs=[pl.BlockSpec((tm, tn), lambda i, j: (i, j)), stat, stat],
        out_shape=[jax.ShapeDtypeStruct((M, N), BF16), jax.ShapeDtypeStruct((M, LANES), F32),
                   jax.ShapeDtypeStruct((M, LANES), F32)],
        scratch_shapes=[pltpu.VMEM((tm, D), BF16)],
        compiler_params=_params("parallel", "arbitrary"),
        name="inproj_a",
    )(x, mod3, g.reshape(1, D), w)


def _log_forget(x, lbl, row):
    m = jnp.max(lbl, axis=0, keepdims=True)
    e = jnp.exp(lbl - m)
    sm = e / jnp.sum(e, axis=0, keepdims=True)
    cum = sm[0:1]
    for r in range(1, row + 1):
        cum = cum + sm[r:r + 1]
    lb = jnp.maximum(cum - sm[0:1], 0.0)
    t = jnp.exp(-jnp.abs(x))
    pos = x >= 0.0
    num = jnp.log(jnp.where(pos, 1.0 + lb * t, t + lb))
    num = jnp.where(pos, num, jnp.maximum(num, x))
    return num - jnp.log(1.0 + t)


def _inproj_b_kernel(x_ref, mod_ref, g_ref, wq_ref, wf_ref, wi_ref, wg_ref, lbl_ref,
                     q_ref, lf_ref, i_ref, sg_ref, h_ref, *, row):
    @pl.when(pl.program_id(1) == 0)
    def _():
        _norm_mod_rows(x_ref, mod_ref, g_ref, h_ref)

    h = h_ref[...]
    q_ref[...] = _silu(_dot(h, wq_ref[...])).astype(q_ref.dtype)
    lf_ref[...] = _log_forget(_dot(h, wf_ref[...]), lbl_ref[...], row)
    i_ref[...] = _dot(h, wi_ref[...]).astype(i_ref.dtype)
    sg_ref[...] = _silu(_dot(h, wg_ref[...])).astype(sg_ref.dtype)


def _inproj_b(x, mod3, g, w, lb_logits, row, seq):
    M, D = x.shape
    R = lb_logits.shape[0]
    tm = _tile(seq, 1024, SUBLANES)
    tn = _tile(D, MXU_WIDTH, LANES)
    per_sec = D // tn
    per_b = seq // tm

    def wspec(sec):
        return pl.BlockSpec((None, D, tn), lambda i, j: (row, 0, sec * per_sec + j))

    out = pl.BlockSpec((tm, tn), lambda i, j: (i, j))
    return pl.pallas_call(
        functools.partial(_inproj_b_kernel, row=row),
        grid=(M // tm, per_sec),
        in_specs=[
            pl.BlockSpec((tm, D), lambda i, j: (i, 0)),
            pl.BlockSpec((None, 3, D), lambda i, j: (i // per_b, 0, 0)),
            pl.BlockSpec((1, D), lambda i, j: (0, 0)),
            wspec(0), wspec(1), wspec(2), wspec(3),
            pl.BlockSpec((R, tn), lambda i, j: (0, j)),
        ],
        out_specs=[out, out, out, out],
        out_shape=[jax.ShapeDtypeStruct((M, D), BF16), jax.ShapeDtypeStruct((M, D), F32),
                   jax.ShapeDtypeStruct((M, D), BF16), jax.ShapeDtypeStruct((M, D), BF16)],
        scratch_shapes=[pltpu.VMEM((tm, D), BF16)],
        compiler_params=_params("parallel", "arbitrary"),
        name="inproj_b",
    )(x, mod3, g.reshape(1, D), w, w, w, w, lb_logits)


def _sgu_outproj_kernel(u_ref, v_ref, s1_ref, s2_ref, lng_ref, lnb_ref, ws_ref, bs_ref, w_ref, x_ref, mod_ref,
                        o_ref, p_ref, *, width):
    tm, D = o_ref.shape

    @pl.when(pl.program_id(1) == 0)
    def _():
        def rows_body(r, carry):
            rows = pl.ds(pl.multiple_of(r * ROW_SLAB, ROW_SLAB), ROW_SLAB)
            o_ref[rows, :] = x_ref[rows, :]
            return carry
        lax.fori_loop(0, tm // ROW_SLAB, rows_body, 0, unroll=4)

    mean = jnp.sum(s1_ref[...], axis=-1, keepdims=True) * (1.0 / width)
    var = jnp.sum(s2_ref[...], axis=-1, keepdims=True) * (1.0 / width) - mean * mean
    rstd = lax.rsqrt(var + EPS)

    t_id = lax.broadcasted_iota(jnp.int32, (A_BLOCK, A_BLOCK), 0) // CHUNK
    s_id = lax.broadcasted_iota(jnp.int32, (A_BLOCK, A_BLOCK), 1) // CHUNK
    w_s = jnp.where(s_id <= t_id, ws_ref[...], 0.0).astype(BF16)
    lng, lnb, bias = lng_ref[...], lnb_ref[...], bs_ref[...]
    for r0 in range(0, tm, A_BLOCK):
        rows = pl.ds(r0, A_BLOCK)
        vn = (v_ref[rows, :].astype(F32) - mean[r0:r0 + A_BLOCK]) * rstd[r0:r0 + A_BLOCK] * lng + lnb
        mixed = _dot(w_s, vn.astype(BF16)) + bias
        p_ref[rows, :] = (u_ref[rows, :].astype(F32) * mixed).astype(BF16)

    p = p_ref[...]
    gate = mod_ref[2:3, :]
    slab = min(D, 2 * MXU_WIDTH)
    for n0 in range(0, D, slab):
        cols = pl.ds(n0, slab)
        o_ref[:, cols] += gate[:, n0:n0 + slab] * _dot(p, w_ref[:, cols])


def _sgu_outproj(z, s1, s2, ln_g, ln_b, w_s, b_s, w, idx, x, mod3, seq):
    M, W2 = z.shape
    W = W2 // 2
    G = w_s.shape[0]
    gd = W // G
    D = w.shape[-1]
    assert gd % LANES == 0
    tm = _tile(seq, 1024, A_BLOCK)
    per_b = seq // tm
    stat = pl.BlockSpec((tm, LANES), lambda i, k: (i, 0))
    return pl.pallas_call(
        functools.partial(_sgu_outproj_kernel, width=W),
        grid=(M // tm, G),
        in_specs=[
            pl.BlockSpec((tm, gd), lambda i, k: (i, k)),
            pl.BlockSpec((tm, gd), lambda i, k: (i, G + k)),
            stat, stat,
            pl.BlockSpec((1, gd), lambda i, k: (0, k)),
            pl.BlockSpec((1, gd), lambda i, k: (0, k)),
            pl.BlockSpec((None, A_BLOCK, A_BLOCK), lambda i, k: (k, 0, 0)),
            pl.BlockSpec((None, A_BLOCK, 1), lambda i, k: (k, 0, 0)),
            pl.BlockSpec((None, gd, D), lambda i, k: (idx, k, 0)),
            pl.BlockSpec((tm, D), lambda i, k: (i, 0)),
            pl.BlockSpec((None, 3, D), lambda i, k: (i // per_b, 0, 0)),
        ],
        out_specs=pl.BlockSpec((tm, D), lambda i, k: (i, 0)),
        out_shape=jax.ShapeDtypeStruct((M, D), F32),
        scratch_shapes=[pltpu.VMEM((tm, gd), BF16)],
        compiler_params=_params("parallel", "arbitrary"),
        name="sgu_outproj",
    )(z, z, s1, s2, ln_g.reshape(1, W), ln_b.reshape(1, W), w_s, b_s[:, :, None], w, x, mod3)


GROUP_CHUNKS = MXU_WIDTH // CHUNK


def _level_sizes():
    sizes, n = [], CHUNK // 2
    while n >= 1:
        sizes.append(n)
        n //= 2
    return sizes


def _decay_tables():
    L = CHUNK
    sizes = _level_sizes()
    D = np.zeros((2 + len(sizes), L, L), np.float32)
    lid = np.full((L, L), -1, np.int32)
    r = np.arange(L)
    for t in range(L):
        D[0, t] = r <= t
        D[1, t] = r > t
        lid[t, t] = 0
    for li, n in enumerate(sizes):
        for t in range(L):
            m = (t // (2 * n)) * 2 * n + n - 1
            if (t // n) % 2 == 1:
                D[2 + li, t] = (r > m) & (r <= t)
                lid[t, m - n + 1:m + 1] = li + 1
            else:
                D[2 + li, t] = (r > t) & (r <= m)
    D = D.reshape(-1, L)
    return np.concatenate([D, D, D], axis=1), lid


def _split3(x):
    hi = x.astype(BF16)
    r1 = x - hi.astype(F32)
    mid = r1.astype(BF16)
    lo = (r1 - mid.astype(F32)).astype(BF16)
    return jnp.concatenate([hi, mid, lo], axis=0)


def _recur_kernel(q_ref, lf_ref, v_ref, g_ref, ng_ref, dcat_ref, lid_ref, o_ref, s_ref):
    @pl.when(pl.program_id(2) == 0)
    def _():
        s_ref[...] = jnp.zeros_like(s_ref)

    ts, dk = q_ref.shape
    n_chunks = ts // CHUNK
    n_lvl = len(_level_sizes())
    gr = lid_ref.shape[0]
    lid = lid_ref[...]
    ng = ng_ref[...]

    lf_wide = jnp.concatenate([lf_ref[pl.ds(c * CHUNK, CHUNK), :] for c in range(n_chunks)], axis=1)
    e = jnp.exp(_dot(dcat_ref[...], _split3(lf_wide)))

    def decay(b):
        return jnp.concatenate(
            [e[b * CHUNK:(b + 1) * CHUNK, c * dk:(c + 1) * dk] for c in range(n_chunks)], axis=0)

    qb = q_ref[...]
    q = qb.astype(F32)
    k = 1.0 - jnp.exp(lf_ref[...])
    vb = v_ref[...]
    qs = (q * decay(0)).astype(BF16)
    ks = (k * decay(1)).astype(BF16)
    ql = [qb] + [(q * decay(2 + li)).astype(BF16) for li in range(n_lvl)]
    kl = [k.astype(BF16)] + [(k * decay(2 + li)).astype(BF16) for li in range(n_lvl)]

    intra = []
    for g0 in range(0, ts, gr):
        rs = slice(g0, g0 + gr)
        sc = jnp.where(lid == 0, _dot_nt(ql[0][rs], kl[0][rs]), 0.0)
        for li in range(1, n_lvl + 1):
            sc = jnp.where(lid == li, _dot_nt(ql[li][rs], kl[li][rs]), sc)
        intra.append(_dot(sc.astype(BF16), vb[rs]))

    upd = [_dot_tn(vb[r0:r0 + CHUNK], ks[r0:r0 + CHUNK]) for r0 in range(0, ts, CHUNK)]
    states = [s_ref[...]]
    for c in range(n_chunks):
        e_last = e[CHUNK - 1:CHUNK, c * dk:(c + 1) * dk]
        states.append(states[c] * e_last + upd[c])
    s_ref[...] = states[n_chunks]
    for c in range(n_chunks):
        r0 = c * CHUNK
        rows = pl.ds(r0, CHUNK)
        o = _dot_nt(qs[r0:r0 + CHUNK], states[c].astype(BF16)) + intra[r0 // gr][r0 % gr:r0 % gr + CHUNK]
        o_ref[rows, :] = (_rms(o) * ng * g_ref[rows, :].astype(F32)).astype(o_ref.dtype)


def _recurrence(q, lf, iv, sg, norm_g, batch, seq):
    M, D = q.shape
    H, dv = norm_g.shape
    ts = _tile(seq, 1024, CHUNK)
    nt = seq // ts
    group = min(GROUP_CHUNKS, ts // CHUNK)
    assert (ts // CHUNK) % group == 0
    dcat, lid = _decay_tables()
    lid_g = np.full((group * CHUNK, group * CHUNK), -1, np.int32)
    for c in range(group):
        lid_g[c * CHUNK:(c + 1) * CHUNK, c * CHUNK:(c + 1) * CHUNK] = lid

    head = pl.BlockSpec((ts, dv), lambda b, h, t: (b * nt + t, h))
    return pl.pallas_call(
        _recur_kernel,
        grid=(batch, H, nt),
        in_specs=[
            head, head, head, head,
            pl.BlockSpec((None, 1, dv), lambda b, h, t: (h, 0, 0)),
            pl.BlockSpec(dcat.shape, lambda b, h, t: (0, 0)),
            pl.BlockSpec(lid_g.shape, lambda b, h, t: (0, 0)),
        ],
        out_specs=head,
        out_shape=jax.ShapeDtypeStruct((M, D), BF16),
        scratch_shapes=[pltpu.VMEM((dv, dv), F32)],
        compiler_params=_params("parallel", "parallel", "arbitrary"),
        name="recurrence",
    )(q, lf, iv, sg, norm_g.reshape(H, 1, dv), jnp.asarray(dcat, BF16), jnp.asarray(lid_g))


def _outproj_kernel(p_ref, w_ref, x_ref, mod_ref, o_ref):
    D = o_ref.shape[1]
    slab = min(D, 2 * MXU_WIDTH)
    gate = mod_ref[2:3, :]
    p = p_ref[...]
    for n0 in range(0, D, slab):
        cols = pl.ds(n0, slab)
        o_ref[:, cols] = x_ref[:, cols] + gate[:, n0:n0 + slab] * _dot(p, w_ref[:, cols])


def _outproj(p, w, idx, x, mod3, seq):
    M, K = p.shape
    D = w.shape[-1]
    tm = _tile(seq, 512, SUBLANES)
    per_b = seq // tm
    return pl.pallas_call(
        _outproj_kernel,
        grid=(M // tm,),
        in_specs=[
            pl.BlockSpec((tm, K), lambda i: (i, 0)),
            pl.BlockSpec((None, K, D), lambda i: (idx, 0, 0)),
            pl.BlockSpec((tm, D), lambda i: (i, 0)),
            pl.BlockSpec((None, 3, D), lambda i: (i // per_b, 0, 0)),
        ],
        out_specs=pl.BlockSpec((tm, D), lambda i: (i, 0)),
        out_shape=jax.ShapeDtypeStruct((M, D), F32),
        compiler_params=_params("parallel"),
        name="outproj",
    )(p, w, x, mod3)


def kernel(x, c, mod_w, mod_b, norm_g, ffn_w1, ffn_w3, ffn_w2, a_w_in, a_ln_g, a_ln_b, a_w_s, a_b_s, a_w_out,
           b_w_in, b_lb_logits, b_norm_g, b_w_out, final_g):
    B, S, D = x.shape
    depth = mod_w.shape[0]
    n_mixers = 2
    assert S % A_BLOCK == 0 and A_BLOCK % CHUNK == 0 and D % LANES == 0
    w1, w3, w2 = ffn_w1.astype(BF16), ffn_w3.astype(BF16), ffn_w2.astype(BF16)
    a_in, a_out = a_w_in.astype(BF16), a_w_out.astype(BF16)
    b_in, b_out = b_w_in.astype(BF16), b_w_out.astype(BF16)

    mod = _mod_all(c, mod_w, mod_b)[:, :B].reshape(depth, B, N_SUB, 3, D)
    xf = x.reshape(B * S, D)
    for layer in range(depth):
        j = layer // n_mixers
        last = layer == depth - 1
        xf = _ffn(xf, mod[layer, :, 0], norm_g[layer, 0], w1, w3, w2, layer, 0, final_g, S, False)
        if layer % n_mixers == 0:
            z, s1, s2 = _inproj_a(xf, mod[layer, :, 1], norm_g[layer, 1], a_in, j, S)
            xf = _sgu_outproj(z, s1, s2, a_ln_g[j], a_ln_b[j], a_w_s[j], a_b_s[j], a_out, j, xf,
                              mod[layer, :, 1], S)
        else:
            q, lf, iv, sg = _inproj_b(xf, mod[layer, :, 1], norm_g[layer, 1], b_in, b_lb_logits, j, S)
            p = _recurrence(q, lf, iv, sg, b_norm_g[j], B, S)
            xf = _outproj(p, b_out, j, xf, mod[layer, :, 1], S)
        xf = _ffn(xf, mod[layer, :, 2], norm_g[layer, 2], w1, w3, w2, layer, 1, final_g, S, last)
    return xf.reshape(B, S, D)
```

```python
import functools

import numpy as np
import jax
import jax.numpy as jnp
from jax import lax
from jax.experimental import pallas as pl
from jax.experimental.pallas import tpu as pltpu

EPS = 1e-6
CHUNK = 64
A_BLOCK = 128
N_SUB = 3
LANES = 128
SUBLANES = 8
MXU_WIDTH = 256
ROW_SLAB = 16
VMEM_LIMIT_BYTES = 60 * 1024 * 1024

F32 = jnp.float32
BF16 = jnp.bfloat16


def _tile(dim, target, align):
    if dim <= target:
        return dim
    t = (target // align) * align
    while t >= align:
        if dim % t == 0:
            return t
        t -= align
    return dim


def _params(*semantics, flags=None):
    return pltpu.CompilerParams(dimension_semantics=semantics, vmem_limit_bytes=VMEM_LIMIT_BYTES, flags=flags)


def _dot(a, b):
    return jnp.dot(a, b, preferred_element_type=F32)


def _dot_nt(a, b):
    return lax.dot_general(a, b, (((1,), (1,)), ((), ())), preferred_element_type=F32)


def _dot_tn(a, b):
    return lax.dot_general(a, b, (((0,), (0,)), ((), ())), preferred_element_type=F32)


def _silu(x):
    return x * jax.nn.sigmoid(x)


def _rms(x):
    return x * lax.rsqrt(jnp.mean(x * x, axis=-1, keepdims=True) + EPS)


def _norm_mod_rows(x_ref, mod_ref, g_ref, h_ref, copy_ref=None):
    gain = g_ref[...] * (1.0 + mod_ref[1:2, :])
    shift = mod_ref[0:1, :]

    def rows_body(r, carry):
        rows = pl.ds(pl.multiple_of(r * ROW_SLAB, ROW_SLAB), ROW_SLAB)
        x = x_ref[rows, :]
        h_ref[rows, :] = (_rms(x) * gain + shift).astype(h_ref.dtype)
        if copy_ref is not None:
            copy_ref[rows, :] = x
        return carry

    lax.fori_loop(0, x_ref.shape[0] // ROW_SLAB, rows_body, 0, unroll=4)


def _mod_kernel(c_ref, w_ref, b_ref, o_ref):
    o_ref[...] = _dot(_silu(c_ref[...]), w_ref[...]) + b_ref[...]


def _mod_all(c, mod_w, mod_b):
    L, D, N = mod_w.shape
    B = c.shape[0]
    Bp = -(-B // SUBLANES) * SUBLANES
    c_pad = jnp.zeros((Bp, D), F32).at[:B].set(c)
    tn = _tile(N, 1024, LANES)
    return pl.pallas_call(
        _mod_kernel,
        grid=(L, N // tn),
        in_specs=[
            pl.BlockSpec((Bp, D), lambda l, j: (0, 0)),
            pl.BlockSpec((None, D, tn), lambda l, j: (l, 0, j)),
            pl.BlockSpec((None, 1, tn), lambda l, j: (l, 0, j)),
        ],
        out_specs=pl.BlockSpec((None, Bp, tn), lambda l, j: (l, 0, j)),
        out_shape=jax.ShapeDtypeStruct((L, Bp, N), F32),
        compiler_params=_params("parallel", "parallel"),
        name="mod",
    )(c_pad, mod_w, mod_b.reshape(L, 1, N))


def _ffn_kernel(x_ref, mod_ref, g_ref, w1_ref, w3_ref, w2_ref, fg_ref, *rest, final_norm, cast_next):
    if cast_next:
        n1_ref, n3_ref, n2_ref, o_ref, c1_ref, c3_ref, c2_ref, h_ref = rest
    else:
        o_ref, h_ref = rest
    j = pl.program_id(1)

    @pl.when(j == 0)
    def _():
        _norm_mod_rows(x_ref, mod_ref, g_ref, h_ref, copy_ref=o_ref)

    if cast_next:
        c1_ref[...] = n1_ref[...].astype(BF16)
        c3_ref[...] = n3_ref[...].astype(BF16)
        c2_ref[...] = n2_ref[...].astype(BF16)
    h = h_ref[...]
    a = _dot(h, w1_ref[...])
    b = _dot(h, w3_ref[...])
    p = (_silu(a) * b).astype(BF16)
    D = o_ref.shape[1]
    slab = min(D, 2 * MXU_WIDTH)
    for n0 in range(0, D, slab):
        cols = pl.ds(n0, slab)
        o_ref[:, cols] += (0.5 * mod_ref[2:3, cols]) * _dot(p, w2_ref[:, cols])

    if final_norm:
        @pl.when(j == pl.num_programs(1) - 1)
        def _():
            o_ref[...] = _rms(o_ref[...]) * fg_ref[...]


def _ffn(x, mod3, g, w_bf16, w_f32, nxt, final_g, seq, final_norm):
    M, D = x.shape
    w1, w3, w2 = w_bf16
    Fd = w1.shape[-1]
    tm = _tile(seq, 1024, SUBLANES)
    tf = _tile(Fd, 512, LANES)
    ni, nj = M // tm, Fd // tf
    per_b = seq // tm
    in_specs = [
        pl.BlockSpec((tm, D), lambda i, j: (i, 0)),
        pl.BlockSpec((None, 3, D), lambda i, j: (i // per_b, 0, 0)),
        pl.BlockSpec((1, D), lambda i, j: (0, 0)),
        pl.BlockSpec((D, tf), lambda i, j: (0, j)),
        pl.BlockSpec((D, tf), lambda i, j: (0, j)),
        pl.BlockSpec((tf, D), lambda i, j: (j, 0)),
        pl.BlockSpec((1, D), lambda i, j: (0, 0)),
    ]
    out_specs = [pl.BlockSpec((tm, D), lambda i, j: (i, 0))]
    out_shape = [jax.ShapeDtypeStruct((M, D), F32)]
    args = [x, mod3, g.reshape(1, D), w1, w3, w2, final_g.reshape(1, D)]
    if nxt is not None:
        nl, nh = nxt
        dr = D // ni
        assert D % ni == 0 and dr % LANES == 0
        in_specs += [
            pl.BlockSpec((None, None, dr, tf), lambda i, j: (nl, nh, i, j)),
            pl.BlockSpec((None, None, dr, tf), lambda i, j: (nl, nh, i, j)),
            pl.BlockSpec((None, None, tf, dr), lambda i, j: (nl, nh, j, i)),
        ]
        out_specs += [
            pl.BlockSpec((dr, tf), lambda i, j: (i, j)),
            pl.BlockSpec((dr, tf), lambda i, j: (i, j)),
            pl.BlockSpec((tf, dr), lambda i, j: (j, i)),
        ]
        out_shape += [jax.ShapeDtypeStruct((D, Fd), BF16), jax.ShapeDtypeStruct((D, Fd), BF16),
                      jax.ShapeDtypeStruct((Fd, D), BF16)]
        args += list(w_f32)
    out = pl.pallas_call(
        functools.partial(_ffn_kernel, final_norm=final_norm, cast_next=nxt is not None),
        grid=(ni, nj),
        in_specs=in_specs,
        out_specs=out_specs,
        out_shape=out_shape,
        scratch_shapes=[pltpu.VMEM((tm, D), BF16)],
        compiler_params=_params("parallel", "arbitrary"),
        name="ffn",
    )(*args)
    return out[0], tuple(out[1:])


def _gelu(x):
    return 0.5 * x * (1.0 + lax.erf(x * np.float32(1.0 / np.sqrt(2.0))))


def _fold_lanes(a):
    out = a[:, 0:LANES]
    for c0 in range(LANES, a.shape[1], LANES):
        out = out + a[:, c0:c0 + LANES]
    return out


def _inproj_a_kernel(x_ref, mod_ref, g_ref, w_ref, o_ref, s1_ref, s2_ref, h_ref, *, first_v):
    j = pl.program_id(1)

    @pl.when(j == 0)
    def _():
        _norm_mod_rows(x_ref, mod_ref, g_ref, h_ref)
        s1_ref[...] = jnp.zeros_like(s1_ref)
        s2_ref[...] = jnp.zeros_like(s2_ref)

    y = _gelu(_dot(h_ref[...], w_ref[...]))
    o_ref[...] = y.astype(o_ref.dtype)
    is_v = (j >= first_v).astype(F32)
    s1_ref[...] += is_v * _fold_lanes(y)
    s2_ref[...] += is_v * _fold_lanes(y * y)


def _inproj_a(x, mod3, g, w, idx, seq):
    M, D = x.shape
    N = w.shape[-1]
    tm = _tile(seq, 1024, SUBLANES)
    tn = _tile(N // 2, 2048, LANES)
    per_b = seq // tm
    stat = pl.BlockSpec((tm, LANES), lambda i, j: (i, 0))
    return pl.pallas_call(
        functools.partial(_inproj_a_kernel, first_v=N // 2 // tn),
        grid=(M // tm, N // tn),
        in_specs=[
            pl.BlockSpec((tm, D), lambda i, j: (i, 0)),
            pl.BlockSpec((None, 3, D), lambda i, j: (i // per_b, 0, 0)),
            pl.BlockSpec((1, D), lambda i, j: (0, 0)),
            pl.BlockSpec((None, D, tn), lambda i, j: (idx, 0, j)),
        ],
        out_specs=[pl.BlockSpec((tm, tn), lambda i, j: (i, j)), stat, stat],
        out_shape=[jax.ShapeDtypeStruct((M, N), BF16), jax.ShapeDtypeStruct((M, LANES), F32),
                   jax.ShapeDtypeStruct((M, LANES), F32)],
        scratch_shapes=[pltpu.VMEM((tm, D), BF16)],
        compiler_params=_params("parallel", "arbitrary"),
        name="inproj_a",
    )(x, mod3, g.reshape(1, D), w)


def _log_forget(x, lbl, row):
    m = jnp.max(lbl, axis=0, keepdims=True)
    e = jnp.exp(lbl - m)
    sm = e / jnp.sum(e, axis=0, keepdims=True)
    cum = sm[0:1]
    for r in range(1, row + 1):
        cum = cum + sm[r:r + 1]
    lb = jnp.maximum(cum - sm[0:1], 0.0)
    t = jnp.exp(-jnp.abs(x))
    pos = x >= 0.0
    num = jnp.log(jnp.where(pos, 1.0 + lb * t, t + lb))
    num = jnp.where(pos, num, jnp.maximum(num, x))
    return num - jnp.log(1.0 + t)


def _inproj_b_kernel(x_ref, mod_ref, g_ref, wq_ref, wf_ref, wi_ref, wg_ref, lbl_ref,
                     q_ref, lf_ref, i_ref, sg_ref, h_ref, *, row):
    @pl.when(pl.program_id(1) == 0)
    def _():
        _norm_mod_rows(x_ref, mod_ref, g_ref, h_ref)

    h = h_ref[...]
    q_ref[...] = _silu(_dot(h, wq_ref[...])).astype(q_ref.dtype)
    lf_ref[...] = _log_forget(_dot(h, wf_ref[...]), lbl_ref[...], row)
    i_ref[...] = _dot(h, wi_ref[...]).astype(i_ref.dtype)
    sg_ref[...] = _silu(_dot(h, wg_ref[...])).astype(sg_ref.dtype)


def _inproj_b(x, mod3, g, w, lb_logits, row, seq):
    M, D = x.shape
    R = lb_logits.shape[0]
    tm = _tile(seq, 1024, SUBLANES)
    tn = _tile(D, MXU_WIDTH, LANES)
    per_sec = D // tn
    per_b = seq // tm

    def wspec(sec):
        return pl.BlockSpec((None, D, tn), lambda i, j: (row, 0, sec * per_sec + j))

    out = pl.BlockSpec((tm, tn), lambda i, j: (i, j))
    return pl.pallas_call(
        functools.partial(_inproj_b_kernel, row=row),
        grid=(M // tm, per_sec),
        in_specs=[
            pl.BlockSpec((tm, D), lambda i, j: (i, 0)),
            pl.BlockSpec((None, 3, D), lambda i, j: (i // per_b, 0, 0)),
            pl.BlockSpec((1, D), lambda i, j: (0, 0)),
            wspec(0), wspec(1), wspec(2), wspec(3),
            pl.BlockSpec((R, tn), lambda i, j: (0, j)),
        ],
        out_specs=[out, out, out, out],
        out_shape=[jax.ShapeDtypeStruct((M, D), BF16), jax.ShapeDtypeStruct((M, D), F32),
                   jax.ShapeDtypeStruct((M, D), BF16), jax.ShapeDtypeStruct((M, D), BF16)],
        scratch_shapes=[pltpu.VMEM((tm, D), BF16)],
        compiler_params=_params("parallel", "arbitrary"),
        name="inproj_b",
    )(x, mod3, g.reshape(1, D), w, w, w, w, lb_logits)


def _sgu_outproj_kernel(u_ref, v_ref, s1_ref, s2_ref, lng_ref, lnb_ref, ws_ref, bs_ref, w_ref, x_ref, mod_ref,
                        o_ref, p_ref, *, width):
    tm, D = o_ref.shape

    @pl.when(pl.program_id(1) == 0)
    def _():
        def rows_body(r, carry):
            rows = pl.ds(pl.multiple_of(r * ROW_SLAB, ROW_SLAB), ROW_SLAB)
            o_ref[rows, :] = x_ref[rows, :]
            return carry
        lax.fori_loop(0, tm // ROW_SLAB, rows_body, 0, unroll=4)

    mean = jnp.sum(s1_ref[...], axis=-1, keepdims=True) * (1.0 / width)
    var = jnp.sum(s2_ref[...], axis=-1, keepdims=True) * (1.0 / width) - mean * mean
    rstd = lax.rsqrt(var + EPS)

    t_id = lax.broadcasted_iota(jnp.int32, (A_BLOCK, A_BLOCK), 0) // CHUNK
    s_id = lax.broadcasted_iota(jnp.int32, (A_BLOCK, A_BLOCK), 1) // CHUNK
    w_s = jnp.where(s_id <= t_id, ws_ref[...], 0.0).astype(BF16)
    lng, lnb, bias = lng_ref[...], lnb_ref[...], bs_ref[...]
    for r0 in range(0, tm, A_BLOCK):
        rows = pl.ds(r0, A_BLOCK)
        vn = (v_ref[rows, :].astype(F32) - mean[r0:r0 + A_BLOCK]) * rstd[r0:r0 + A_BLOCK] * lng + lnb
        mixed = _dot(w_s, vn.astype(BF16)) + bias
        p_ref[rows, :] = (u_ref[rows, :].astype(F32) * mixed).astype(BF16)

    p = p_ref[...]
    gate = mod_ref[2:3, :]
    slab = min(D, 2 * MXU_WIDTH)
    for n0 in range(0, D, slab):
        cols = pl.ds(n0, slab)
        o_ref[:, cols] += gate[:, n0:n0 + slab] * _dot(p, w_ref[:, cols])


def _sgu_outproj(z, s1, s2, ln_g, ln_b, w_s, b_s, w, idx, x, mod3, seq):
    M, W2 = z.shape
    W = W2 // 2
    G = w_s.shape[0]
    gd = W // G
    D = w.shape[-1]
    assert gd % LANES == 0
    tm = _tile(seq, 1024, A_BLOCK)
    per_b = seq // tm
    stat = pl.BlockSpec((tm, LANES), lambda i, k: (i, 0))
    return pl.pallas_call(
        functools.partial(_sgu_outproj_kernel, width=W),
        grid=(M // tm, G),
        in_specs=[
            pl.BlockSpec((tm, gd), lambda i, k: (i, k)),
            pl.BlockSpec((tm, gd), lambda i, k: (i, G + k)),
            stat, stat,
            pl.BlockSpec((1, gd), lambda i, k: (0, k)),
            pl.BlockSpec((1, gd), lambda i, k: (0, k)),
            pl.BlockSpec((None, A_BLOCK, A_BLOCK), lambda i, k: (k, 0, 0)),
            pl.BlockSpec((None, A_BLOCK, 1), lambda i, k: (k, 0, 0)),
            pl.BlockSpec((None, gd, D), lambda i, k: (idx, k, 0)),
            pl.BlockSpec((tm, D), lambda i, k: (i, 0)),
            pl.BlockSpec((None, 3, D), lambda i, k: (i // per_b, 0, 0)),
        ],
        out_specs=pl.BlockSpec((tm, D), lambda i, k: (i, 0)),
        out_shape=jax.ShapeDtypeStruct((M, D), F32),
        scratch_shapes=[pltpu.VMEM((tm, gd), BF16)],
        compiler_params=_params("parallel", "arbitrary"),
        name="sgu_outproj",
    )(z, z, s1, s2, ln_g.reshape(1, W), ln_b.reshape(1, W), w_s, b_s[:, :, None], w, x, mod3)


GROUP_CHUNKS = MXU_WIDTH // CHUNK


def _level_sizes():
    sizes, n = [], CHUNK // 2
    while n >= 1:
        sizes.append(n)
        n //= 2
    return sizes


def _decay_tables():
    L = CHUNK
    sizes = _level_sizes()
    D = np.zeros((2 + len(sizes), L, L), np.float32)
    lid = np.full((L, L), -1, np.int32)
    r = np.arange(L)
    for t in range(L):
        D[0, t] = r <= t
        D[1, t] = r > t
        lid[t, t] = 0
    for li, n in enumerate(sizes):
        for t in range(L):
            m = (t // (2 * n)) * 2 * n + n - 1
            if (t // n) % 2 == 1:
                D[2 + li, t] = (r > m) & (r <= t)
                lid[t, m - n + 1:m + 1] = li + 1
            else:
                D[2 + li, t] = (r > t) & (r <= m)
    D = D.reshape(-1, L)
    return np.concatenate([D, D, D], axis=1), lid


def _split3(x):
    hi = x.astype(BF16)
    r1 = x - hi.astype(F32)
    mid = r1.astype(BF16)
    lo = (r1 - mid.astype(F32)).astype(BF16)
    return jnp.concatenate([hi, mid, lo], axis=0)


def _recur_kernel(q_ref, lf_ref, v_ref, g_ref, ng_ref, dcat_ref, lid_ref, o_ref, s_ref):
    @pl.when(pl.program_id(2) == 0)
    def _():
        s_ref[...] = jnp.zeros_like(s_ref)

    ts, dk = q_ref.shape
    n_chunks = ts // CHUNK
    n_lvl = len(_level_sizes())
    gr = lid_ref.shape[0]
    lid = lid_ref[...]
    ng = ng_ref[...]

    lf_wide = jnp.concatenate([lf_ref[pl.ds(c * CHUNK, CHUNK), :] for c in range(n_chunks)], axis=1)
    e = jnp.exp(_dot(dcat_ref[...], _split3(lf_wide)))

    def decay(b):
        return jnp.concatenate(
            [e[b * CHUNK:(b + 1) * CHUNK, c * dk:(c + 1) * dk] for c in range(n_chunks)], axis=0)

    qb = q_ref[...]
    q = qb.astype(F32)
    k = 1.0 - jnp.exp(lf_ref[...])
    vb = v_ref[...]
    qs = (q * decay(0)).astype(BF16)
    ks = (k * decay(1)).astype(BF16)
    ql = [qb] + [(q * decay(2 + li)).astype(BF16) for li in range(n_lvl)]
    kl = [k.astype(BF16)] + [(k * decay(2 + li)).astype(BF16) for li in range(n_lvl)]

    intra = []
    for g0 in range(0, ts, gr):
        rs = slice(g0, g0 + gr)
        sc = jnp.where(lid == 0, _dot_nt(ql[0][rs], kl[0][rs]), 0.0)
        for li in range(1, n_lvl + 1):
            sc = jnp.where(lid == li, _dot_nt(ql[li][rs], kl[li][rs]), sc)
        intra.append(_dot(sc.astype(BF16), vb[rs]))

    upd = [_dot_tn(vb[r0:r0 + CHUNK], ks[r0:r0 + CHUNK]) for r0 in range(0, ts, CHUNK)]
    states = [s_ref[...]]
    for c in range(n_chunks):
        e_last = e[CHUNK - 1:CHUNK, c * dk:(c + 1) * dk]
        states.append(states[c] * e_last + upd[c])
    s_ref[...] = states[n_chunks]
    for c in range(n_chunks):
        r0 = c * CHUNK
        rows = pl.ds(r0, CHUNK)
        o = _dot_nt(qs[r0:r0 + CHUNK], states[c].astype(BF16)) + intra[r0 // gr][r0 % gr:r0 % gr + CHUNK]
        o_ref[rows, :] = (_rms(o) * ng * g_ref[rows, :].astype(F32)).astype(o_ref.dtype)


def _recurrence(q, lf, iv, sg, norm_g, batch, seq):
    M, D = q.shape
    H, dv = norm_g.shape
    ts = _tile(seq, 1024, CHUNK)
    nt = seq // ts
    group = min(GROUP_CHUNKS, ts // CHUNK)
    assert (ts // CHUNK) % group == 0
    dcat, lid = _decay_tables()
    lid_g = np.full((group * CHUNK, group * CHUNK), -1, np.int32)
    for c in range(group):
        lid_g[c * CHUNK:(c + 1) * CHUNK, c * CHUNK:(c + 1) * CHUNK] = lid

    head = pl.BlockSpec((ts, dv), lambda b, h, t: (b * nt + t, h))
    return pl.pallas_call(
        _recur_kernel,
        grid=(batch, H, nt),
        in_specs=[
            head, head, head, head,
            pl.BlockSpec((None, 1, dv), lambda b, h, t: (h, 0, 0)),
            pl.BlockSpec(dcat.shape, lambda b, h, t: (0, 0)),
            pl.BlockSpec(lid_g.shape, lambda b, h, t: (0, 0)),
        ],
        out_specs=head,
        out_shape=jax.ShapeDtypeStruct((M, D), BF16),
        scratch_shapes=[pltpu.VMEM((dv, dv), F32)],
        compiler_params=_params("parallel", "parallel", "arbitrary"),
        name="recurrence",
    )(q, lf, iv, sg, norm_g.reshape(H, 1, dv), jnp.asarray(dcat, BF16), jnp.asarray(lid_g))


def _outproj_kernel(p_ref, w_ref, x_ref, mod_ref, o_ref):
    D = o_ref.shape[1]
    slab = min(D, 2 * MXU_WIDTH)
    gate = mod_ref[2:3, :]
    p = p_ref[...]
    for n0 in range(0, D, slab):
        cols = pl.ds(n0, slab)
        o_ref[:, cols] = x_ref[:, cols] + gate[:, n0:n0 + slab] * _dot(p, w_ref[:, cols])


def _outproj(p, w, idx, x, mod3, seq):
    M, K = p.shape
    D = w.shape[-1]
    tm = _tile(seq, 512, SUBLANES)
    per_b = seq // tm
    return pl.pallas_call(
        _outproj_kernel,
        grid=(M // tm,),
        in_specs=[
            pl.BlockSpec((tm, K), lambda i: (i, 0)),
            pl.BlockSpec((None, K, D), lambda i: (idx, 0, 0)),
            pl.BlockSpec((tm, D), lambda i: (i, 0)),
            pl.BlockSpec((None, 3, D), lambda i: (i // per_b, 0, 0)),
        ],
        out_specs=pl.BlockSpec((tm, D), lambda i: (i, 0)),
        out_shape=jax.ShapeDtypeStruct((M, D), F32),
        compiler_params=_params("parallel"),
        name="outproj",
    )(p, w, x, mod3)


def kernel(x, c, mod_w, mod_b, norm_g, ffn_w1, ffn_w3, ffn_w2, a_w_in, a_ln_g, a_ln_b, a_w_s, a_b_s, a_w_out,
           b_w_in, b_lb_logits, b_norm_g, b_w_out, final_g):
    B, S, D = x.shape
    depth = mod_w.shape[0]
    n_mixers = 2
    assert S % A_BLOCK == 0 and A_BLOCK % CHUNK == 0 and D % LANES == 0
    ffn_f32 = (ffn_w1, ffn_w3, ffn_w2)
    ffn_order = [(layer, half) for layer in range(depth) for half in range(2)]
    w_ffn = tuple(w[0, 0].astype(BF16) for w in ffn_f32)
    a_in, a_out = a_w_in.astype(BF16), a_w_out.astype(BF16)
    b_in, b_out = b_w_in.astype(BF16), b_w_out.astype(BF16)

    mod = _mod_all(c, mod_w, mod_b)[:, :B].reshape(depth, B, N_SUB, 3, D)
    xf = x.reshape(B * S, D)
    def ffn(xf, w_ffn, layer, half):
        pos = ffn_order.index((layer, half))
        nxt = ffn_order[pos + 1] if pos + 1 < len(ffn_order) else None
        return _ffn(xf, mod[layer, :, 2 * half], norm_g[layer, 2 * half], w_ffn, ffn_f32, nxt, final_g, S,
                    nxt is None)

    for layer in range(depth):
        j = layer // n_mixers
        xf, w_ffn = ffn(xf, w_ffn, layer, 0)
        if layer % n_mixers == 0:
            z, s1, s2 = _inproj_a(xf, mod[layer, :, 1], norm_g[layer, 1], a_in, j, S)
            xf = _sgu_outproj(z, s1, s2, a_ln_g[j], a_ln_b[j], a_w_s[j], a_b_s[j], a_out, j, xf,
                              mod[layer, :, 1], S)
        else:
            q, lf, iv, sg = _inproj_b(xf, mod[layer, :, 1], norm_g[layer, 1], b_in, b_lb_logits, j, S)
            p = _recurrence(q, lf, iv, sg, b_norm_g[j], B, S)
            xf = _outproj(p, b_out, j, xf, mod[layer, :, 1], S)
        xf, w_ffn = ffn(xf, w_ffn, layer, 1)
    return xf.reshape(B, S, D)
```

```python
import functools

import numpy as np
import jax
import jax.numpy as jnp
from jax import lax
from jax.experimental import pallas as pl
from jax.experimental.pallas import tpu as pltpu

EPS = 1e-6
CHUNK = 64
A_BLOCK = 128
N_SUB = 3
LANES = 128
SUBLANES = 8
MXU_WIDTH = 256
ROW_SLAB = 16
VMEM_LIMIT_BYTES = 60 * 1024 * 1024

F32 = jnp.float32
BF16 = jnp.bfloat16


def _tile(dim, target, align):
    if dim <= target:
        return dim
    t = (target // align) * align
    while t >= align:
        if dim % t == 0:
            return t
        t -= align
    return dim


def _params(*semantics, flags=None):
    return pltpu.CompilerParams(dimension_semantics=semantics, vmem_limit_bytes=VMEM_LIMIT_BYTES, flags=flags)


def _dot(a, b):
    return jnp.dot(a, b, preferred_element_type=F32)


def _dot_nt(a, b):
    return lax.dot_general(a, b, (((1,), (1,)), ((), ())), preferred_element_type=F32)


def _dot_tn(a, b):
    return lax.dot_general(a, b, (((0,), (0,)), ((), ())), preferred_element_type=F32)


def _silu(x):
    return x * jax.nn.sigmoid(x)


def _rms(x):
    return x * lax.rsqrt(jnp.mean(x * x, axis=-1, keepdims=True) + EPS)


def _norm_mod_rows(x_ref, mod_ref, g_ref, h_ref, copy_ref=None):
    gain = g_ref[...] * (1.0 + mod_ref[1:2, :])
    shift = mod_ref[0:1, :]

    def rows_body(r, carry):
        rows = pl.ds(pl.multiple_of(r * ROW_SLAB, ROW_SLAB), ROW_SLAB)
        x = x_ref[rows, :]
        h_ref[rows, :] = (_rms(x) * gain + shift).astype(h_ref.dtype)
        if copy_ref is not None:
            copy_ref[rows, :] = x
        return carry

    lax.fori_loop(0, x_ref.shape[0] // ROW_SLAB, rows_body, 0, unroll=8 if copy_ref is not None else 16)


def _mod_kernel(c_ref, w_ref, b_ref, o_ref):
    o_ref[...] = _dot(_silu(c_ref[...]), w_ref[...]) + b_ref[...]


def _mod_all(c, mod_w, mod_b):
    L, D, N = mod_w.shape
    B = c.shape[0]
    Bp = -(-B // SUBLANES) * SUBLANES
    c_pad = jnp.zeros((Bp, D), F32).at[:B].set(c)
    tn = _tile(N, 1024, LANES)
    return pl.pallas_call(
        _mod_kernel,
        grid=(L, N // tn),
        in_specs=[
            pl.BlockSpec((Bp, D), lambda l, j: (0, 0)),
            pl.BlockSpec((None, D, tn), lambda l, j: (l, 0, j)),
            pl.BlockSpec((None, 1, tn), lambda l, j: (l, 0, j)),
        ],
        out_specs=pl.BlockSpec((None, Bp, tn), lambda l, j: (l, 0, j)),
        out_shape=jax.ShapeDtypeStruct((L, Bp, N), F32),
        compiler_params=_params("parallel", "parallel"),
        name="mod",
    )(c_pad, mod_w, mod_b.reshape(L, 1, N))


def _ffn_kernel(x_ref, mod_ref, g_ref, w1_ref, w3_ref, w2_ref, fg_ref, *rest, final_norm, cast_next):
    if cast_next:
        n1_ref, n3_ref, n2_ref, o_ref, c1_ref, c3_ref, c2_ref, h_ref = rest
    else:
        o_ref, h_ref = rest
    j = pl.program_id(1)

    @pl.when(j == 0)
    def _():
        _norm_mod_rows(x_ref, mod_ref, g_ref, h_ref, copy_ref=o_ref)

    if cast_next:
        c1_ref[...] = n1_ref[...].astype(BF16)
        c3_ref[...] = n3_ref[...].astype(BF16)
        c2_ref[...] = n2_ref[...].astype(BF16)
    h = h_ref[...]
    a = _dot(h, w1_ref[...])
    b = _dot(h, w3_ref[...])
    p = (_silu(a) * b).astype(BF16)
    D = o_ref.shape[1]
    slab = min(D, 2 * MXU_WIDTH)
    for n0 in range(0, D, slab):
        cols = pl.ds(n0, slab)
        o_ref[:, cols] += (0.5 * mod_ref[2:3, cols]) * _dot(p, w2_ref[:, cols])

    if final_norm:
        @pl.when(j == pl.num_programs(1) - 1)
        def _():
            o_ref[...] = _rms(o_ref[...]) * fg_ref[...]


def _ffn(x, mod3, g, w_bf16, w_f32, nxt, final_g, seq, final_norm):
    M, D = x.shape
    w1, w3, w2 = w_bf16
    Fd = w1.shape[-1]
    tm = _tile(seq, 1024, SUBLANES)
    tf = _tile(Fd, 512, LANES)
    ni, nj = M // tm, Fd // tf
    per_b = seq // tm
    in_specs = [
        pl.BlockSpec((tm, D), lambda i, j: (i, 0)),
        pl.BlockSpec((None, 3, D), lambda i, j: (i // per_b, 0, 0)),
        pl.BlockSpec((1, D), lambda i, j: (0, 0)),
        pl.BlockSpec((D, tf), lambda i, j: (0, j)),
        pl.BlockSpec((D, tf), lambda i, j: (0, j)),
        pl.BlockSpec((tf, D), lambda i, j: (j, 0)),
        pl.BlockSpec((1, D), lambda i, j: (0, 0)),
    ]
    out_specs = [pl.BlockSpec((tm, D), lambda i, j: (i, 0))]
    out_shape = [jax.ShapeDtypeStruct((M, D), F32)]
    args = [x, mod3, g.reshape(1, D), w1, w3, w2, final_g.reshape(1, D)]
    if nxt is not None:
        nl, nh = nxt
        dr = D // ni
        assert D % ni == 0 and dr % LANES == 0
        in_specs += [
            pl.BlockSpec((None, None, dr, tf), lambda i, j: (nl, nh, i, j)),
            pl.BlockSpec((None, None, dr, tf), lambda i, j: (nl, nh, i, j)),
            pl.BlockSpec((None, None, tf, dr), lambda i, j: (nl, nh, j, i)),
        ]
        out_specs += [
            pl.BlockSpec((dr, tf), lambda i, j: (i, j)),
            pl.BlockSpec((dr, tf), lambda i, j: (i, j)),
            pl.BlockSpec((tf, dr), lambda i, j: (j, i)),
        ]
        out_shape += [jax.ShapeDtypeStruct((D, Fd), BF16), jax.ShapeDtypeStruct((D, Fd), BF16),
                      jax.ShapeDtypeStruct((Fd, D), BF16)]
        args += list(w_f32)
    out = pl.pallas_call(
        functools.partial(_ffn_kernel, final_norm=final_norm, cast_next=nxt is not None),
        grid=(ni, nj),
        in_specs=in_specs,
        out_specs=out_specs,
        out_shape=out_shape,
        scratch_shapes=[pltpu.VMEM((tm, D), BF16)],
        compiler_params=_params("parallel", "arbitrary"),
        name="ffn",
    )(*args)
    return out[0], tuple(out[1:])


def _gelu(x):
    return 0.5 * x * (1.0 + lax.erf(x * np.float32(1.0 / np.sqrt(2.0))))


def _fold_lanes(a):
    out = a[:, 0:LANES]
    for c0 in range(LANES, a.shape[1], LANES):
        out = out + a[:, c0:c0 + LANES]
    return out


def _inproj_a_kernel(x_ref, mod_ref, g_ref, w_ref, o_ref, s1_ref, s2_ref, h_ref, *, first_v):
    j = pl.program_id(1)

    @pl.when(j == 0)
    def _():
        _norm_mod_rows(x_ref, mod_ref, g_ref, h_ref)
        s1_ref[...] = jnp.zeros_like(s1_ref)
        s2_ref[...] = jnp.zeros_like(s2_ref)

    y = _gelu(_dot(h_ref[...], w_ref[...]))
    o_ref[...] = y.astype(o_ref.dtype)
    is_v = (j >= first_v).astype(F32)
    s1_ref[...] += is_v * _fold_lanes(y)
    s2_ref[...] += is_v * _fold_lanes(y * y)


def _inproj_a(x, mod3, g, w, seq):
    M, D = x.shape
    N = w.shape[-1]
    tm = _tile(seq, 1024, SUBLANES)
    tn = _tile(N // 2, 2048, LANES)
    per_b = seq // tm
    stat = pl.BlockSpec((tm, LANES), lambda i, j: (i, 0))
    return pl.pallas_call(
        functools.partial(_inproj_a_kernel, first_v=N // 2 // tn),
        grid=(M // tm, N // tn),
        in_specs=[
            pl.BlockSpec((tm, D), lambda i, j: (i, 0)),
            pl.BlockSpec((None, 3, D), lambda i, j: (i // per_b, 0, 0)),
            pl.BlockSpec((1, D), lambda i, j: (0, 0)),
            pl.BlockSpec((D, tn), lambda i, j: (0, j)),
        ],
        out_specs=[pl.BlockSpec((tm, tn), lambda i, j: (i, j)), stat, stat],
        out_shape=[jax.ShapeDtypeStruct((M, N), BF16), jax.ShapeDtypeStruct((M, LANES), F32),
                   jax.ShapeDtypeStruct((M, LANES), F32)],
        scratch_shapes=[pltpu.VMEM((tm, D), BF16)],
        compiler_params=_params("parallel", "arbitrary"),
        name="inproj_a",
    )(x, mod3, g.reshape(1, D), w)


def _log_forget(x, lbl, row):
    m = jnp.max(lbl, axis=0, keepdims=True)
    e = jnp.exp(lbl - m)
    sm = e / jnp.sum(e, axis=0, keepdims=True)
    cum = sm[0:1]
    for r in range(1, row + 1):
        cum = cum + sm[r:r + 1]
    lb = jnp.maximum(cum - sm[0:1], 0.0)
    t = jnp.exp(-jnp.abs(x))
    pos = x >= 0.0
    num = jnp.log(jnp.where(pos, 1.0 + lb * t, t + lb))
    num = jnp.where(pos, num, jnp.maximum(num, x))
    return num - jnp.log(1.0 + t)


def _inproj_b_kernel(*refs, row, n_cast):
    x_ref, mod_ref, g_ref, wq_ref, wf_ref, wi_ref, wg_ref, lbl_ref = refs[:8]
    cast_in = refs[8:8 + n_cast]
    q_ref, lf_ref, i_ref, sg_ref = refs[8 + n_cast:12 + n_cast]
    cast_out = refs[12 + n_cast:12 + 2 * n_cast]
    h_ref = refs[-1]

    @pl.when(pl.program_id(1) == 0)
    def _():
        _norm_mod_rows(x_ref, mod_ref, g_ref, h_ref)

    for src, dst in zip(cast_in, cast_out):
        dst[...] = src[...].astype(BF16)
    h = h_ref[...]
    q_ref[...] = _silu(_dot(h, wq_ref[...])).astype(q_ref.dtype)
    lf_ref[...] = _log_forget(_dot(h, wf_ref[...]), lbl_ref[...], row)
    i_ref[...] = _dot(h, wi_ref[...]).astype(i_ref.dtype)
    sg_ref[...] = _silu(_dot(h, wg_ref[...])).astype(sg_ref.dtype)


def _inproj_b(x, mod3, g, w, lb_logits, row, seq, casts):
    M, D = x.shape
    R = lb_logits.shape[0]
    tm = _tile(seq, 1024, SUBLANES)
    tn = _tile(D, MXU_WIDTH, LANES)
    per_sec = D // tn
    per_b = seq // tm
    ni, nj = M // tm, per_sec

    def wspec(sec):
        return pl.BlockSpec((D, tn), lambda i, j: (0, sec * per_sec + j))

    out = pl.BlockSpec((tm, tn), lambda i, j: (i, j))
    cast_in, cast_out, cast_shape = [], [], []
    for stack, idx in casts:
        _, A, Bc = stack.shape
        rows = A // (ni * nj)
        assert A % (ni * nj) == 0 and rows % (2 * SUBLANES) == 0
        cast_in.append(pl.BlockSpec((None, rows, Bc), lambda i, j, idx=idx: (idx, i * nj + j, 0)))
        cast_out.append(pl.BlockSpec((rows, Bc), lambda i, j: (i * nj + j, 0)))
        cast_shape.append(jax.ShapeDtypeStruct((A, Bc), BF16))
    res = pl.pallas_call(
        functools.partial(_inproj_b_kernel, row=row, n_cast=len(casts)),
        grid=(ni, nj),
        in_specs=[
            pl.BlockSpec((tm, D), lambda i, j: (i, 0)),
            pl.BlockSpec((None, 3, D), lambda i, j: (i // per_b, 0, 0)),
            pl.BlockSpec((1, D), lambda i, j: (0, 0)),
            wspec(0), wspec(1), wspec(2), wspec(3),
            pl.BlockSpec((R, tn), lambda i, j: (0, j)),
        ] + cast_in,
        out_specs=[out, out, out, out] + cast_out,
        out_shape=[jax.ShapeDtypeStruct((M, D), BF16), jax.ShapeDtypeStruct((M, D), F32),
                   jax.ShapeDtypeStruct((M, D), BF16), jax.ShapeDtypeStruct((M, D), BF16)] + cast_shape,
        scratch_shapes=[pltpu.VMEM((tm, D), BF16)],
        compiler_params=_params("parallel", "arbitrary"),
        name="inproj_b",
    )(x, mod3, g.reshape(1, D), w, w, w, w, lb_logits, *[stack for stack, _ in casts])
    return res[:4], res[4:]


def _sgu_outproj_kernel(u_ref, v_ref, s1_ref, s2_ref, lng_ref, lnb_ref, ws_ref, bs_ref, w_ref, x_ref, mod_ref,
                        o_ref, p_ref, *, width):
    tm, D = o_ref.shape

    @pl.when(pl.program_id(1) == 0)
    def _():
        def rows_body(r, carry):
            rows = pl.ds(pl.multiple_of(r * ROW_SLAB, ROW_SLAB), ROW_SLAB)
            o_ref[rows, :] = x_ref[rows, :]
            return carry
        lax.fori_loop(0, tm // ROW_SLAB, rows_body, 0, unroll=4)

    mean = jnp.sum(s1_ref[...], axis=-1, keepdims=True) * (1.0 / width)
    var = jnp.sum(s2_ref[...], axis=-1, keepdims=True) * (1.0 / width) - mean * mean
    rstd = lax.rsqrt(var + EPS)

    t_id = lax.broadcasted_iota(jnp.int32, (A_BLOCK, A_BLOCK), 0) // CHUNK
    s_id = lax.broadcasted_iota(jnp.int32, (A_BLOCK, A_BLOCK), 1) // CHUNK
    w_s = jnp.where(s_id <= t_id, ws_ref[...], 0.0).astype(BF16)
    lng, lnb, bias = lng_ref[...], lnb_ref[...], bs_ref[...]
    for r0 in range(0, tm, A_BLOCK):
        rows = pl.ds(r0, A_BLOCK)
        vn = (v_ref[rows, :].astype(F32) - mean[r0:r0 + A_BLOCK]) * rstd[r0:r0 + A_BLOCK] * lng + lnb
        mixed = _dot(w_s, vn.astype(BF16)) + bias
        p_ref[rows, :] = (u_ref[rows, :].astype(F32) * mixed).astype(BF16)

    p = p_ref[...]
    gate = mod_ref[2:3, :]
    slab = min(D, 2 * MXU_WIDTH)
    for n0 in range(0, D, slab):
        cols = pl.ds(n0, slab)
        o_ref[:, cols] += gate[:, n0:n0 + slab] * _dot(p, w_ref[:, cols])


def _sgu_outproj(z, s1, s2, ln_g, ln_b, w_s, b_s, w, x, mod3, seq):
    M, W2 = z.shape
    W = W2 // 2
    G = w_s.shape[0]
    gd = W // G
    D = w.shape[-1]
    assert gd % LANES == 0
    tm = _tile(seq, 1024, A_BLOCK)
    per_b = seq // tm
    stat = pl.BlockSpec((tm, LANES), lambda i, k: (i, 0))
    return pl.pallas_call(
        functools.partial(_sgu_outproj_kernel, width=W),
        grid=(M // tm, G),
        in_specs=[
            pl.BlockSpec((tm, gd), lambda i, k: (i, k)),
            pl.BlockSpec((tm, gd), lambda i, k: (i, G + k)),
            stat, stat,
            pl.BlockSpec((1, gd), lambda i, k: (0, k)),
            pl.BlockSpec((1, gd), lambda i, k: (0, k)),
            pl.BlockSpec((None, A_BLOCK, A_BLOCK), lambda i, k: (k, 0, 0)),
            pl.BlockSpec((None, A_BLOCK, 1), lambda i, k: (k, 0, 0)),
            pl.BlockSpec((gd, D), lambda i, k: (k, 0)),
            pl.BlockSpec((tm, D), lambda i, k: (i, 0)),
            pl.BlockSpec((None, 3, D), lambda i, k: (i // per_b, 0, 0)),
        ],
        out_specs=pl.BlockSpec((tm, D), lambda i, k: (i, 0)),
        out_shape=jax.ShapeDtypeStruct((M, D), F32),
        scratch_shapes=[pltpu.VMEM((tm, gd), BF16)],
        compiler_params=_params("parallel", "arbitrary"),
        name="sgu_outproj",
    )(z, z, s1, s2, ln_g.reshape(1, W), ln_b.reshape(1, W), w_s, b_s[:, :, None], w, x, mod3)


GROUP_CHUNKS = MXU_WIDTH // CHUNK


def _level_sizes():
    sizes, n = [], CHUNK // 2
    while n >= 1:
        sizes.append(n)
        n //= 2
    return sizes


def _decay_tables():
    L = CHUNK
    sizes = _level_sizes()
    D = np.zeros((2 + len(sizes), L, L), np.float32)
    lid = np.full((L, L), -1, np.int32)
    r = np.arange(L)
    for t in range(L):
        D[0, t] = r <= t
        D[1, t] = r > t
        lid[t, t] = 0
    for li, n in enumerate(sizes):
        for t in range(L):
            m = (t // (2 * n)) * 2 * n + n - 1
            if (t // n) % 2 == 1:
                D[2 + li, t] = (r > m) & (r <= t)
                lid[t, m - n + 1:m + 1] = li + 1
            else:
                D[2 + li, t] = (r > t) & (r <= m)
    D = D.reshape(-1, L)
    return np.concatenate([D, D, D], axis=1), lid


def _split3(x):
    hi = x.astype(BF16)
    r1 = x - hi.astype(F32)
    mid = r1.astype(BF16)
    lo = (r1 - mid.astype(F32)).astype(BF16)
    return jnp.concatenate([hi, mid, lo], axis=0)


def _recur_kernel(q_ref, lf_ref, v_ref, g_ref, ng_ref, dcat_ref, lid_ref, o_ref, s_ref):
    @pl.when(pl.program_id(2) == 0)
    def _():
        s_ref[...] = jnp.zeros_like(s_ref)

    ts, dk = q_ref.shape
    n_chunks = ts // CHUNK
    n_lvl = len(_level_sizes())
    gr = lid_ref.shape[0]
    lid = lid_ref[...]
    ng = ng_ref[...]

    lf_wide = jnp.concatenate([lf_ref[pl.ds(c * CHUNK, CHUNK), :] for c in range(n_chunks)], axis=1)
    e = jnp.exp(_dot(dcat_ref[...], _split3(lf_wide)))

    def decay(b):
        return jnp.concatenate(
            [e[b * CHUNK:(b + 1) * CHUNK, c * dk:(c + 1) * dk] for c in range(n_chunks)], axis=0)

    qb = q_ref[...]
    q = qb.astype(F32)
    k = 1.0 - jnp.exp(lf_ref[...])
    vb = v_ref[...]
    qs = (q * decay(0)).astype(BF16)
    ks = (k * decay(1)).astype(BF16)
    ql = [qb] + [(q * decay(2 + li)).astype(BF16) for li in range(n_lvl)]
    kl = [k.astype(BF16)] + [(k * decay(2 + li)).astype(BF16) for li in range(n_lvl)]

    intra = []
    for g0 in range(0, ts, gr):
        rs = slice(g0, g0 + gr)
        sc = jnp.where(lid == 0, _dot_nt(ql[0][rs], kl[0][rs]), 0.0)
        for li in range(1, n_lvl + 1):
            sc = jnp.where(lid == li, _dot_nt(ql[li][rs], kl[li][rs]), sc)
        intra.append(_dot(sc.astype(BF16), vb[rs]))

    upd = [_dot_tn(vb[r0:r0 + CHUNK], ks[r0:r0 + CHUNK]) for r0 in range(0, ts, CHUNK)]
    states = [s_ref[...]]
    for c in range(n_chunks):
        e_last = e[CHUNK - 1:CHUNK, c * dk:(c + 1) * dk]
        states.append(states[c] * e_last + upd[c])
    s_ref[...] = states[n_chunks]
    for c in range(n_chunks):
        r0 = c * CHUNK
        rows = pl.ds(r0, CHUNK)
        o = _dot_nt(qs[r0:r0 + CHUNK], states[c].astype(BF16)) + intra[r0 // gr][r0 % gr:r0 % gr + CHUNK]
        o_ref[rows, :] = (_rms(o) * ng * g_ref[rows, :].astype(F32)).astype(o_ref.dtype)


def _recurrence(q, lf, iv, sg, norm_g, batch, seq):
    M, D = q.shape
    H, dv = norm_g.shape
    ts = _tile(seq, 1024, CHUNK)
    nt = seq // ts
    group = min(GROUP_CHUNKS, ts // CHUNK)
    assert (ts // CHUNK) % group == 0
    dcat, lid = _decay_tables()
    lid_g = np.full((group * CHUNK, group * CHUNK), -1, np.int32)
    for c in range(group):
        lid_g[c * CHUNK:(c + 1) * CHUNK, c * CHUNK:(c + 1) * CHUNK] = lid

    head = pl.BlockSpec((ts, dv), lambda b, h, t: (b * nt + t, h))
    return pl.pallas_call(
        _recur_kernel,
        grid=(batch, H, nt),
        in_specs=[
            head, head, head, head,
            pl.BlockSpec((None, 1, dv), lambda b, h, t: (h, 0, 0)),
            pl.BlockSpec(dcat.shape, lambda b, h, t: (0, 0)),
            pl.BlockSpec(lid_g.shape, lambda b, h, t: (0, 0)),
        ],
        out_specs=head,
        out_shape=jax.ShapeDtypeStruct((M, D), BF16),
        scratch_shapes=[pltpu.VMEM((dv, dv), F32)],
        compiler_params=_params("parallel", "parallel", "arbitrary"),
        name="recurrence",
    )(q, lf, iv, sg, norm_g.reshape(H, 1, dv), jnp.asarray(dcat, BF16), jnp.asarray(lid_g))


def _outproj_kernel(p_ref, w_ref, x_ref, mod_ref, o_ref):
    D = o_ref.shape[1]
    slab = min(D, 2 * MXU_WIDTH)
    gate = mod_ref[2:3, :]
    p = p_ref[...]
    for n0 in range(0, D, slab):
        cols = pl.ds(n0, slab)
        o_ref[:, cols] = x_ref[:, cols] + gate[:, n0:n0 + slab] * _dot(p, w_ref[:, cols])


def _outproj(p, w, x, mod3, seq):
    M, K = p.shape
    D = w.shape[-1]
    tm = _tile(seq, 512, SUBLANES)
    per_b = seq // tm
    return pl.pallas_call(
        _outproj_kernel,
        grid=(M // tm,),
        in_specs=[
            pl.BlockSpec((tm, K), lambda i: (i, 0)),
            pl.BlockSpec((K, D), lambda i: (0, 0)),
            pl.BlockSpec((tm, D), lambda i: (i, 0)),
            pl.BlockSpec((None, 3, D), lambda i: (i // per_b, 0, 0)),
        ],
        out_specs=pl.BlockSpec((tm, D), lambda i: (i, 0)),
        out_shape=jax.ShapeDtypeStruct((M, D), F32),
        compiler_params=_params("parallel"),
        name="outproj",
    )(p, w, x, mod3)


def kernel(x, c, mod_w, mod_b, norm_g, ffn_w1, ffn_w3, ffn_w2, a_w_in, a_ln_g, a_ln_b, a_w_s, a_b_s, a_w_out,
           b_w_in, b_lb_logits, b_norm_g, b_w_out, final_g):
    B, S, D = x.shape
    depth = mod_w.shape[0]
    n_mixers = 2
    assert S % A_BLOCK == 0 and A_BLOCK % CHUNK == 0 and D % LANES == 0
    ffn_f32 = (ffn_w1, ffn_w3, ffn_w2)
    ffn_order = [(layer, half) for layer in range(depth) for half in range(2)]
    w_ffn = tuple(w[0, 0].astype(BF16) for w in ffn_f32)
    n_a, n_b = a_w_in.shape[0], b_w_in.shape[0]
    a_in, a_out = {0: a_w_in[0].astype(BF16)}, {0: a_w_out[0].astype(BF16)}
    b_in, b_out = ({0: b_w_in[0].astype(BF16)} if n_b else {}), {}

    mod = _mod_all(c, mod_w, mod_b)[:, :B].reshape(depth, B, N_SUB, 3, D)
    xf = x.reshape(B * S, D)

    def ffn(xf, w_ffn, layer, half):
        pos = ffn_order.index((layer, half))
        nxt = ffn_order[pos + 1] if pos + 1 < len(ffn_order) else None
        return _ffn(xf, mod[layer, :, 2 * half], norm_g[layer, 2 * half], w_ffn, ffn_f32, nxt, final_g, S,
                    nxt is None)

    for layer in range(depth):
        j = layer // n_mixers
        xf, w_ffn = ffn(xf, w_ffn, layer, 0)
        if layer % n_mixers == 0:
            z, s1, s2 = _inproj_a(xf, mod[layer, :, 1], norm_g[layer, 1], a_in[j], S)
            xf = _sgu_outproj(z, s1, s2, a_ln_g[j], a_ln_b[j], a_w_s[j], a_b_s[j], a_out[j], xf,
                              mod[layer, :, 1], S)
        else:
            casts, dests = [(b_w_out, j)], [(b_out, j)]
            if j + 1 < n_a:
                casts += [(a_w_in, j + 1), (a_w_out, j + 1)]
                dests += [(a_in, j + 1), (a_out, j + 1)]
            if j + 1 < n_b:
                casts += [(b_w_in, j + 1)]
                dests += [(b_in, j + 1)]
            (q, lf, iv, sg), cast = _inproj_b(xf, mod[layer, :, 1], norm_g[layer, 1], b_in[j], b_lb_logits, j, S,
                                              casts)
            for (table, key), w_cast in zip(dests, cast):
                table[key] = w_cast
            p = _recurrence(q, lf, iv, sg, b_norm_g[j], B, S)
            xf = _outproj(p, b_out[j], xf, mod[layer, :, 1], S)
        xf, w_ffn = ffn(xf, w_ffn, layer, 1)
    return xf.reshape(B, S, D)
```

```python
import functools

import numpy as np
import jax
import jax.numpy as jnp
from jax import lax
from jax.experimental import pallas as pl
from jax.experimental.pallas import tpu as pltpu

EPS = 1e-6
CHUNK = 64
A_BLOCK = 128
N_SUB = 3
LANES = 128
SUBLANES = 8
MXU_WIDTH = 256
ROW_SLAB = 16
VMEM_LIMIT_BYTES = 60 * 1024 * 1024

F32 = jnp.float32
BF16 = jnp.bfloat16


def _tile(dim, target, align):
    if dim <= target:
        return dim
    t = (target // align) * align
    while t >= align:
        if dim % t == 0:
            return t
        t -= align
    return dim


def _params(*semantics, flags=None):
    return pltpu.CompilerParams(dimension_semantics=semantics, vmem_limit_bytes=VMEM_LIMIT_BYTES, flags=flags)


def _dot(a, b):
    return jnp.dot(a, b, preferred_element_type=F32)


def _dot_nt(a, b):
    return lax.dot_general(a, b, (((1,), (1,)), ((), ())), preferred_element_type=F32)


def _dot_tn(a, b):
    return lax.dot_general(a, b, (((0,), (0,)), ((), ())), preferred_element_type=F32)


def _silu(x):
    return x * jax.nn.sigmoid(x)


def _rms(x):
    return x * lax.rsqrt(jnp.mean(x * x, axis=-1, keepdims=True) + EPS)


def _norm_mod_rows(x_ref, mod_ref, g_ref, h_ref, copy_ref=None):
    gain = g_ref[...] * (1.0 + mod_ref[1:2, :])
    shift = mod_ref[0:1, :]

    def rows_body(r, carry):
        rows = pl.ds(pl.multiple_of(r * ROW_SLAB, ROW_SLAB), ROW_SLAB)
        x = x_ref[rows, :]
        h_ref[rows, :] = (_rms(x) * gain + shift).astype(h_ref.dtype)
        if copy_ref is not None:
            copy_ref[rows, :] = x
        return carry

    lax.fori_loop(0, x_ref.shape[0] // ROW_SLAB, rows_body, 0, unroll=8 if copy_ref is not None else 16)


def _mod_kernel(c_ref, w_ref, b_ref, o_ref):
    o_ref[...] = _dot(_silu(c_ref[...]), w_ref[...]) + b_ref[...]


def _mod_all(c, mod_w, mod_b):
    L, D, N = mod_w.shape
    B = c.shape[0]
    Bp = -(-B // SUBLANES) * SUBLANES
    c_pad = jnp.zeros((Bp, D), F32).at[:B].set(c)
    tn = _tile(N, 2048, LANES)
    return pl.pallas_call(
        _mod_kernel,
        grid=(L, N // tn),
        in_specs=[
            pl.BlockSpec((Bp, D), lambda l, j: (0, 0)),
            pl.BlockSpec((None, D, tn), lambda l, j: (l, 0, j)),
            pl.BlockSpec((None, 1, tn), lambda l, j: (l, 0, j)),
        ],
        out_specs=pl.BlockSpec((None, Bp, tn), lambda l, j: (l, 0, j)),
        out_shape=jax.ShapeDtypeStruct((L, Bp, N), F32),
        compiler_params=_params("parallel", "parallel"),
        name="mod",
    )(c_pad, mod_w, mod_b.reshape(L, 1, N))


def _ffn_kernel(x_ref, mod_ref, g_ref, w1_ref, w3_ref, w2_ref, fg_ref, *rest, final_norm, cast_next):
    if cast_next:
        n1_ref, n3_ref, n2_ref, o_ref, c1_ref, c3_ref, c2_ref, h_ref = rest
    else:
        o_ref, h_ref = rest
    j = pl.program_id(1)

    @pl.when(j == 0)
    def _():
        _norm_mod_rows(x_ref, mod_ref, g_ref, h_ref, copy_ref=o_ref)

    if cast_next:
        c1_ref[...] = n1_ref[...].astype(BF16)
        c3_ref[...] = n3_ref[...].astype(BF16)
        c2_ref[...] = n2_ref[...].astype(BF16)
    h = h_ref[...]
    a = _dot(h, w1_ref[...])
    b = _dot(h, w3_ref[...])
    p = (_silu(a) * b).astype(BF16)
    D = o_ref.shape[1]
    slab = min(D, 2 * MXU_WIDTH)
    for n0 in range(0, D, slab):
        cols = pl.ds(n0, slab)
        o_ref[:, cols] += (0.5 * mod_ref[2:3, cols]) * _dot(p, w2_ref[:, cols])

    if final_norm:
        @pl.when(j == pl.num_programs(1) - 1)
        def _():
            o_ref[...] = _rms(o_ref[...]) * fg_ref[...]


def _ffn(x, mod3, g, w_bf16, w_f32, nxt, final_g, seq, final_norm):
    M, D = x.shape
    w1, w3, w2 = w_bf16
    Fd = w1.shape[-1]
    tm = _tile(seq, 1024, SUBLANES)
    tf = _tile(Fd, 512, LANES)
    ni, nj = M // tm, Fd // tf
    per_b = seq // tm
    in_specs = [
        pl.BlockSpec((tm, D), lambda i, j: (i, 0)),
        pl.BlockSpec((None, 3, D), lambda i, j: (i // per_b, 0, 0)),
        pl.BlockSpec((1, D), lambda i, j: (0, 0)),
        pl.BlockSpec((D, tf), lambda i, j: (0, j)),
        pl.BlockSpec((D, tf), lambda i, j: (0, j)),
        pl.BlockSpec((tf, D), lambda i, j: (j, 0)),
        pl.BlockSpec((1, D), lambda i, j: (0, 0)),
    ]
    out_specs = [pl.BlockSpec((tm, D), lambda i, j: (i, 0))]
    out_shape = [jax.ShapeDtypeStruct((M, D), F32)]
    args = [x, mod3, g.reshape(1, D), w1, w3, w2, final_g.reshape(1, D)]
    if nxt is not None:
        nl, nh = nxt
        dr = D // ni
        assert D % ni == 0 and dr % LANES == 0
        in_specs += [
            pl.BlockSpec((None, None, dr, tf), lambda i, j: (nl, nh, i, j)),
            pl.BlockSpec((None, None, dr, tf), lambda i, j: (nl, nh, i, j)),
            pl.BlockSpec((None, None, tf, dr), lambda i, j: (nl, nh, j, i)),
        ]
        out_specs += [
            pl.BlockSpec((dr, tf), lambda i, j: (i, j)),
            pl.BlockSpec((dr, tf), lambda i, j: (i, j)),
            pl.BlockSpec((tf, dr), lambda i, j: (j, i)),
        ]
        out_shape += [jax.ShapeDtypeStruct((D, Fd), BF16), jax.ShapeDtypeStruct((D, Fd), BF16),
                      jax.ShapeDtypeStruct((Fd, D), BF16)]
        args += list(w_f32)
    out = pl.pallas_call(
        functools.partial(_ffn_kernel, final_norm=final_norm, cast_next=nxt is not None),
        grid=(ni, nj),
        in_specs=in_specs,
        out_specs=out_specs,
        out_shape=out_shape,
        scratch_shapes=[pltpu.VMEM((tm, D), BF16)],
        compiler_params=_params("parallel", "arbitrary"),
        name="ffn",
    )(*args)
    return out[0], tuple(out[1:])


def _gelu(x):
    return 0.5 * x * (1.0 + lax.erf(x * np.float32(1.0 / np.sqrt(2.0))))


def _fold_lanes(a):
    out = a[:, 0:LANES]
    for c0 in range(LANES, a.shape[1], LANES):
        out = out + a[:, c0:c0 + LANES]
    return out


def _inproj_a_kernel(x_ref, mod_ref, g_ref, w_ref, o_ref, s1_ref, s2_ref, h_ref, *, first_v):
    j = pl.program_id(1)

    @pl.when(j == 0)
    def _():
        _norm_mod_rows(x_ref, mod_ref, g_ref, h_ref)
        s1_ref[...] = jnp.zeros_like(s1_ref)
        s2_ref[...] = jnp.zeros_like(s2_ref)

    y = _gelu(_dot(h_ref[...], w_ref[...]))
    o_ref[...] = y.astype(o_ref.dtype)
    is_v = (j >= first_v).astype(F32)
    s1_ref[...] += is_v * _fold_lanes(y)
    s2_ref[...] += is_v * _fold_lanes(y * y)


def _inproj_a(x, mod3, g, w, seq):
    M, D = x.shape
    N = w.shape[-1]
    tm = _tile(seq, 1024, SUBLANES)
    tn = _tile(N // 2, 2048, LANES)
    per_b = seq // tm
    stat = pl.BlockSpec((tm, LANES), lambda i, j: (i, 0))
    return pl.pallas_call(
        functools.partial(_inproj_a_kernel, first_v=N // 2 // tn),
        grid=(M // tm, N // tn),
        in_specs=[
            pl.BlockSpec((tm, D), lambda i, j: (i, 0)),
            pl.BlockSpec((None, 3, D), lambda i, j: (i // per_b, 0, 0)),
            pl.BlockSpec((1, D), lambda i, j: (0, 0)),
            pl.BlockSpec((D, tn), lambda i, j: (0, j)),
        ],
        out_specs=[pl.BlockSpec((tm, tn), lambda i, j: (i, j)), stat, stat],
        out_shape=[jax.ShapeDtypeStruct((M, N), BF16), jax.ShapeDtypeStruct((M, LANES), F32),
                   jax.ShapeDtypeStruct((M, LANES), F32)],
        scratch_shapes=[pltpu.VMEM((tm, D), BF16)],
        compiler_params=_params("parallel", "arbitrary"),
        name="inproj_a",
    )(x, mod3, g.reshape(1, D), w)


def _log_forget(x, lbl, row):
    m = jnp.max(lbl, axis=0, keepdims=True)
    e = jnp.exp(lbl - m)
    sm = e / jnp.sum(e, axis=0, keepdims=True)
    cum = sm[0:1]
    for r in range(1, row + 1):
        cum = cum + sm[r:r + 1]
    lb = jnp.maximum(cum - sm[0:1], 0.0)
    t = jnp.exp(-jnp.abs(x))
    pos = x >= 0.0
    num = jnp.log(jnp.where(pos, 1.0 + lb * t, t + lb))
    num = jnp.where(pos, num, jnp.maximum(num, x))
    return num - jnp.log(1.0 + t)


def _inproj_b_kernel(*refs, row, n_cast):
    x_ref, mod_ref, g_ref, wq_ref, wf_ref, wi_ref, wg_ref, lbl_ref = refs[:8]
    cast_in = refs[8:8 + n_cast]
    q_ref, lf_ref, i_ref, sg_ref = refs[8 + n_cast:12 + n_cast]
    cast_out = refs[12 + n_cast:12 + 2 * n_cast]
    h_ref = refs[-1]

    @pl.when(pl.program_id(1) == 0)
    def _():
        _norm_mod_rows(x_ref, mod_ref, g_ref, h_ref)

    for src, dst in zip(cast_in, cast_out):
        dst[...] = src[...].astype(BF16)
    h = h_ref[...]
    q_ref[...] = _silu(_dot(h, wq_ref[...])).astype(q_ref.dtype)
    lf_ref[...] = _log_forget(_dot(h, wf_ref[...]), lbl_ref[...], row)
    i_ref[...] = _dot(h, wi_ref[...]).astype(i_ref.dtype)
    sg_ref[...] = _silu(_dot(h, wg_ref[...])).astype(sg_ref.dtype)


def _inproj_b(x, mod3, g, w, lb_logits, row, seq, casts):
    M, D = x.shape
    R = lb_logits.shape[0]
    tm = _tile(seq, 1024, SUBLANES)
    tn = _tile(D, MXU_WIDTH, LANES)
    per_sec = D // tn
    per_b = seq // tm
    ni, nj = M // tm, per_sec

    def wspec(sec):
        return pl.BlockSpec((D, tn), lambda i, j: (0, sec * per_sec + j))

    out = pl.BlockSpec((tm, tn), lambda i, j: (i, j))
    cast_in, cast_out, cast_shape = [], [], []
    for stack, idx in casts:
        _, A, Bc = stack.shape
        rows = A // (ni * nj)
        assert A % (ni * nj) == 0 and rows % (2 * SUBLANES) == 0
        cast_in.append(pl.BlockSpec((None, rows, Bc), lambda i, j, idx=idx: (idx, i * nj + j, 0)))
        cast_out.append(pl.BlockSpec((rows, Bc), lambda i, j: (i * nj + j, 0)))
        cast_shape.append(jax.ShapeDtypeStruct((A, Bc), BF16))
    res = pl.pallas_call(
        functools.partial(_inproj_b_kernel, row=row, n_cast=len(casts)),
        grid=(ni, nj),
        in_specs=[
            pl.BlockSpec((tm, D), lambda i, j: (i, 0)),
            pl.BlockSpec((None, 3, D), lambda i, j: (i // per_b, 0, 0)),
            pl.BlockSpec((1, D), lambda i, j: (0, 0)),
            wspec(0), wspec(1), wspec(2), wspec(3),
            pl.BlockSpec((R, tn), lambda i, j: (0, j)),
        ] + cast_in,
        out_specs=[out, out, out, out] + cast_out,
        out_shape=[jax.ShapeDtypeStruct((M, D), BF16), jax.ShapeDtypeStruct((M, D), F32),
                   jax.ShapeDtypeStruct((M, D), BF16), jax.ShapeDtypeStruct((M, D), BF16)] + cast_shape,
        scratch_shapes=[pltpu.VMEM((tm, D), BF16)],
        compiler_params=_params("parallel", "arbitrary"),
        name="inproj_b",
    )(x, mod3, g.reshape(1, D), w, w, w, w, lb_logits, *[stack for stack, _ in casts])
    return res[:4], res[4:]


def _sgu_outproj_kernel(u_ref, v_ref, s1_ref, s2_ref, lng_ref, lnb_ref, ws_ref, bs_ref, w_ref, x_ref, mod_ref,
                        o_ref, p_ref, *, width):
    tm, D = o_ref.shape

    @pl.when(pl.program_id(1) == 0)
    def _():
        def rows_body(r, carry):
            rows = pl.ds(pl.multiple_of(r * ROW_SLAB, ROW_SLAB), ROW_SLAB)
            o_ref[rows, :] = x_ref[rows, :]
            return carry
        lax.fori_loop(0, tm // ROW_SLAB, rows_body, 0, unroll=4)

    mean = jnp.sum(s1_ref[...], axis=-1, keepdims=True) * (1.0 / width)
    var = jnp.sum(s2_ref[...], axis=-1, keepdims=True) * (1.0 / width) - mean * mean
    rstd = lax.rsqrt(var + EPS)

    t_id = lax.broadcasted_iota(jnp.int32, (A_BLOCK, A_BLOCK), 0) // CHUNK
    s_id = lax.broadcasted_iota(jnp.int32, (A_BLOCK, A_BLOCK), 1) // CHUNK
    w_s = jnp.where(s_id <= t_id, ws_ref[...], 0.0).astype(BF16)
    lng, lnb, bias = lng_ref[...], lnb_ref[...], bs_ref[...]
    gate = mod_ref[2:3, :]
    slab = min(D, 2 * MXU_WIDTH)
    half = max(A_BLOCK, tm // 2)
    for h0 in range(0, tm, half):
        for r0 in range(h0, h0 + half, A_BLOCK):
            rows = pl.ds(r0, A_BLOCK)
            vn = (v_ref[rows, :].astype(F32) - mean[r0:r0 + A_BLOCK]) * rstd[r0:r0 + A_BLOCK] * lng + lnb
            mixed = _dot(w_s, vn.astype(BF16)) + bias
            p_ref[rows, :] = (u_ref[rows, :].astype(F32) * mixed).astype(BF16)
        hrows = pl.ds(h0, half)
        p = p_ref[hrows, :]
        for n0 in range(0, D, slab):
            cols = pl.ds(n0, slab)
            o_ref[hrows, cols] += gate[:, n0:n0 + slab] * _dot(p, w_ref[:, cols])


def _sgu_outproj(z, s1, s2, ln_g, ln_b, w_s, b_s, w, x, mod3, seq):
    M, W2 = z.shape
    W = W2 // 2
    G = w_s.shape[0]
    gd = W // G
    D = w.shape[-1]
    assert gd % LANES == 0
    tm = _tile(seq, 1024, A_BLOCK)
    per_b = seq // tm
    stat = pl.BlockSpec((tm, LANES), lambda i, k: (i, 0))
    return pl.pallas_call(
        functools.partial(_sgu_outproj_kernel, width=W),
        grid=(M // tm, G),
        in_specs=[
            pl.BlockSpec((tm, gd), lambda i, k: (i, k)),
            pl.BlockSpec((tm, gd), lambda i, k: (i, G + k)),
            stat, stat,
            pl.BlockSpec((1, gd), lambda i, k: (0, k)),
            pl.BlockSpec((1, gd), lambda i, k: (0, k)),
            pl.BlockSpec((None, A_BLOCK, A_BLOCK), lambda i, k: (k, 0, 0)),
            pl.BlockSpec((None, A_BLOCK, 1), lambda i, k: (k, 0, 0)),
            pl.BlockSpec((gd, D), lambda i, k: (k, 0)),
            pl.BlockSpec((tm, D), lambda i, k: (i, 0)),
            pl.BlockSpec((None, 3, D), lambda i, k: (i // per_b, 0, 0)),
        ],
        out_specs=pl.BlockSpec((tm, D), lambda i, k: (i, 0)),
        out_shape=jax.ShapeDtypeStruct((M, D), F32),
        scratch_shapes=[pltpu.VMEM((tm, gd), BF16)],
        compiler_params=_params("parallel", "arbitrary"),
        name="sgu_outproj",
    )(z, z, s1, s2, ln_g.reshape(1, W), ln_b.reshape(1, W), w_s, b_s[:, :, None], w, x, mod3)


GROUP_CHUNKS = MXU_WIDTH // CHUNK


def _level_sizes():
    sizes, n = [], CHUNK // 2
    while n >= 1:
        sizes.append(n)
        n //= 2
    return sizes


def _decay_tables():
    L = CHUNK
    sizes = _level_sizes()
    D = np.zeros((2 + len(sizes), L, L), np.float32)
    lid = np.full((L, L), -1, np.int32)
    r = np.arange(L)
    for t in range(L):
        D[0, t] = r <= t
        D[1, t] = r > t
        lid[t, t] = 0
    for li, n in enumerate(sizes):
        for t in range(L):
            m = (t // (2 * n)) * 2 * n + n - 1
            if (t // n) % 2 == 1:
                D[2 + li, t] = (r > m) & (r <= t)
                lid[t, m - n + 1:m + 1] = li + 1
            else:
                D[2 + li, t] = (r > t) & (r <= m)
    D = D.reshape(-1, L)
    return np.concatenate([D, D, D], axis=1), lid


def _split3(x):
    hi = x.astype(BF16)
    r1 = x - hi.astype(F32)
    mid = r1.astype(BF16)
    lo = (r1 - mid.astype(F32)).astype(BF16)
    return jnp.concatenate([hi, mid, lo], axis=0)


def _recur_kernel(q_ref, lf_ref, v_ref, g_ref, ng_ref, dcat_ref, lid_ref, o_ref, s_ref):
    @pl.when(pl.program_id(2) == 0)
    def _():
        s_ref[...] = jnp.zeros_like(s_ref)

    ts, dk = q_ref.shape
    n_chunks = ts // CHUNK
    n_lvl = len(_level_sizes())
    gr = lid_ref.shape[0]
    lid = lid_ref[...]
    ng = ng_ref[...]

    lf_wide = jnp.concatenate([lf_ref[pl.ds(c * CHUNK, CHUNK), :] for c in range(n_chunks)], axis=1)
    e = jnp.exp(_dot(dcat_ref[...], _split3(lf_wide)))

    def decay(b):
        return jnp.concatenate(
            [e[b * CHUNK:(b + 1) * CHUNK, c * dk:(c + 1) * dk] for c in range(n_chunks)], axis=0)

    qb = q_ref[...]
    q = qb.astype(F32)
    k = 1.0 - jnp.exp(lf_ref[...])
    vb = v_ref[...]
    qs = (q * decay(0)).astype(BF16)
    ks = (k * decay(1)).astype(BF16)
    ql = [qb] + [(q * decay(2 + li)).astype(BF16) for li in range(n_lvl)]
    kl = [k.astype(BF16)] + [(k * decay(2 + li)).astype(BF16) for li in range(n_lvl)]

    intra = []
    for g0 in range(0, ts, gr):
        rs = slice(g0, g0 + gr)
        sc = jnp.where(lid == 0, _dot_nt(ql[0][rs], kl[0][rs]), 0.0)
        for li in range(1, n_lvl + 1):
            sc = jnp.where(lid == li, _dot_nt(ql[li][rs], kl[li][rs]), sc)
        intra.append(_dot(sc.astype(BF16), vb[rs]))

    upd = [_dot_tn(vb[r0:r0 + CHUNK], ks[r0:r0 + CHUNK]) for r0 in range(0, ts, CHUNK)]
    states = [s_ref[...]]
    for c in range(n_chunks):
        e_last = e[CHUNK - 1:CHUNK, c * dk:(c + 1) * dk]
        states.append(states[c] * e_last + upd[c])
    s_ref[...] = states[n_chunks]
    for c in range(n_chunks):
        r0 = c * CHUNK
        rows = pl.ds(r0, CHUNK)
        o = _dot_nt(qs[r0:r0 + CHUNK], states[c].astype(BF16)) + intra[r0 // gr][r0 % gr:r0 % gr + CHUNK]
        o_ref[rows, :] = (_rms(o) * ng * g_ref[rows, :].astype(F32)).astype(o_ref.dtype)


def _recurrence(q, lf, iv, sg, norm_g, batch, seq):
    M, D = q.shape
    H, dv = norm_g.shape
    ts = _tile(seq, 2048, CHUNK)
    nt = seq // ts
    group = min(GROUP_CHUNKS, ts // CHUNK)
    assert (ts // CHUNK) % group == 0
    dcat, lid = _decay_tables()
    lid_g = np.full((group * CHUNK, group * CHUNK), -1, np.int32)
    for c in range(group):
        lid_g[c * CHUNK:(c + 1) * CHUNK, c * CHUNK:(c + 1) * CHUNK] = lid

    head = pl.BlockSpec((ts, dv), lambda b, h, t: (b * nt + t, h))
    return pl.pallas_call(
        _recur_kernel,
        grid=(batch, H, nt),
        in_specs=[
            head, head, head, head,
            pl.BlockSpec((None, 1, dv), lambda b, h, t: (h, 0, 0)),
            pl.BlockSpec(dcat.shape, lambda b, h, t: (0, 0)),
            pl.BlockSpec(lid_g.shape, lambda b, h, t: (0, 0)),
        ],
        out_specs=head,
        out_shape=jax.ShapeDtypeStruct((M, D), BF16),
        scratch_shapes=[pltpu.VMEM((dv, dv), F32)],
        compiler_params=_params("parallel", "parallel", "arbitrary"),
        name="recurrence",
    )(q, lf, iv, sg, norm_g.reshape(H, 1, dv), jnp.asarray(dcat, BF16), jnp.asarray(lid_g))


def _outproj_kernel(p_ref, w_ref, x_ref, mod_ref, o_ref):
    D = o_ref.shape[1]
    slab = min(D, 2 * MXU_WIDTH)
    gate = mod_ref[2:3, :]
    p = p_ref[...]
    for n0 in range(0, D, slab):
        cols = pl.ds(n0, slab)
        o_ref[:, cols] = x_ref[:, cols] + gate[:, n0:n0 + slab] * _dot(p, w_ref[:, cols])


def _outproj(p, w, x, mod3, seq):
    M, K = p.shape
    D = w.shape[-1]
    tm = _tile(seq, 512, SUBLANES)
    per_b = seq // tm
    return pl.pallas_call(
        _outproj_kernel,
        grid=(M // tm,),
        in_specs=[
            pl.BlockSpec((tm, K), lambda i: (i, 0)),
            pl.BlockSpec((K, D), lambda i: (0, 0)),
            pl.BlockSpec((tm, D), lambda i: (i, 0)),
            pl.BlockSpec((None, 3, D), lambda i: (i // per_b, 0, 0)),
        ],
        out_specs=pl.BlockSpec((tm, D), lambda i: (i, 0)),
        out_shape=jax.ShapeDtypeStruct((M, D), F32),
        compiler_params=_params("parallel"),
        name="outproj",
    )(p, w, x, mod3)


def kernel(x, c, mod_w, mod_b, norm_g, ffn_w1, ffn_w3, ffn_w2, a_w_in, a_ln_g, a_ln_b, a_w_s, a_b_s, a_w_out,
           b_w_in, b_lb_logits, b_norm_g, b_w_out, final_g):
    B, S, D = x.shape
    depth = mod_w.shape[0]
    n_mixers = 2
    assert S % A_BLOCK == 0 and A_BLOCK % CHUNK == 0 and D % LANES == 0
    ffn_f32 = (ffn_w1, ffn_w3, ffn_w2)
    ffn_order = [(layer, half) for layer in range(depth) for half in range(2)]
    w_ffn = tuple(w[0, 0].astype(BF16) for w in ffn_f32)
    n_a, n_b = a_w_in.shape[0], b_w_in.shape[0]
    a_in, a_out = {0: a_w_in[0].astype(BF16)}, {0: a_w_out[0].astype(BF16)}
    b_in, b_out = ({0: b_w_in[0].astype(BF16)} if n_b else {}), {}

    mod = _mod_all(c, mod_w, mod_b)[:, :B].reshape(depth, B, N_SUB, 3, D)
    xf = x.reshape(B * S, D)

    def ffn(xf, w_ffn, layer, half):
        pos = ffn_order.index((layer, half))
        nxt = ffn_order[pos + 1] if pos + 1 < len(ffn_order) else None
        return _ffn(xf, mod[layer, :, 2 * half], norm_g[layer, 2 * half], w_ffn, ffn_f32, nxt, final_g, S,
                    nxt is None)

    for layer in range(depth):
        j = layer // n_mixers
        xf, w_ffn = ffn(xf, w_ffn, layer, 0)
        if layer % n_mixers == 0:
            z, s1, s2 = _inproj_a(xf, mod[layer, :, 1], norm_g[layer, 1], a_in[j], S)
            xf = _sgu_outproj(z, s1, s2, a_ln_g[j], a_ln_b[j], a_w_s[j], a_b_s[j], a_out[j], xf,
                              mod[layer, :, 1], S)
        else:
            casts, dests = [(b_w_out, j)], [(b_out, j)]
            if j + 1 < n_a:
                casts += [(a_w_in, j + 1), (a_w_out, j + 1)]
                dests += [(a_in, j + 1), (a_out, j + 1)]
            if j + 1 < n_b:
                casts += [(b_w_in, j + 1)]
                dests += [(b_in, j + 1)]
            (q, lf, iv, sg), cast = _inproj_b(xf, mod[layer, :, 1], norm_g[layer, 1], b_in[j], b_lb_logits, j, S,
                                              casts)
            for (table, key), w_cast in zip(dests, cast):
                table[key] = w_cast
            p = _recurrence(q, lf, iv, sg, b_norm_g[j], B, S)
            xf = _outproj(p, b_out[j], xf, mod[layer, :, 1], S)
        xf, w_ffn = ffn(xf, w_ffn, layer, 1)
    return xf.reshape(B, S, D)
```

```python
import functools

import numpy as np
import jax
import jax.numpy as jnp
from jax import lax
from jax.experimental import pallas as pl
from jax.experimental.pallas import tpu as pltpu

EPS = 1e-6
CHUNK = 64
A_BLOCK = 128
N_SUB = 3
LANES = 128
SUBLANES = 8
MXU_WIDTH = 256
ROW_SLAB = 16
VMEM_LIMIT_BYTES = 60 * 1024 * 1024

F32 = jnp.float32
BF16 = jnp.bfloat16


def _tile(dim, target, align):
    if dim <= target:
        return dim
    t = (target // align) * align
    while t >= align:
        if dim % t == 0:
            return t
        t -= align
    return dim


def _params(*semantics):
    return pltpu.CompilerParams(dimension_semantics=semantics, vmem_limit_bytes=VMEM_LIMIT_BYTES)


def _dot(a, b):
    return jnp.dot(a, b, preferred_element_type=F32)


def _dot_nt(a, b):
    return lax.dot_general(a, b, (((1,), (1,)), ((), ())), preferred_element_type=F32)


def _dot_tn(a, b):
    return lax.dot_general(a, b, (((0,), (0,)), ((), ())), preferred_element_type=F32)


def _silu(x):
    return (0.5 * x) * (1.0 + jnp.tanh(0.5 * x))


def _rms(x):
    return x * lax.rsqrt(jnp.mean(x * x, axis=-1, keepdims=True) + EPS)


def _norm_mod_rows(x_ref, mod_ref, g_ref, h_ref, copy_ref=None):
    gain = g_ref[...] * (1.0 + mod_ref[1:2, :])
    shift = mod_ref[0:1, :]

    def rows_body(r, carry):
        rows = pl.ds(pl.multiple_of(r * ROW_SLAB, ROW_SLAB), ROW_SLAB)
        x = x_ref[rows, :]
        h_ref[rows, :] = (_rms(x) * gain + shift).astype(h_ref.dtype)
        if copy_ref is not None:
            copy_ref[rows, :] = x
        return carry

    lax.fori_loop(0, x_ref.shape[0] // ROW_SLAB, rows_body, 0, unroll=8 if copy_ref is not None else 16)


def _lookahead_plan(tm, nj):
    assert nj >= 2 and tm % ROW_SLAB == 0
    slabs = -(-(tm // ROW_SLAB) // (nj - 1))
    return slabs, slabs * ROW_SLAB


def _lookahead_x_spec(tm, D, ni):
    return pl.BlockSpec((tm, D), lambda i, j: (jnp.minimum(i + jnp.minimum(j, 1), ni - 1), 0))


def _norm_mod_hidden(x_ref, modn_ref, g_ref, h_nxt, slabs):
    j = pl.program_id(1)
    tm = x_ref.shape[0]
    last = tm // ROW_SLAB - 1
    gain = g_ref[...] * (1.0 + modn_ref[1:2, :])
    shift = modn_ref[0:1, :]
    for r in range(slabs):
        s = jnp.minimum((j - 1) * slabs + r, last)
        x0 = pl.multiple_of(jnp.where(j >= 1, s, r) * ROW_SLAB, ROW_SLAB)
        h0 = pl.multiple_of(jnp.where(j >= 1, s * ROW_SLAB, tm + r * ROW_SLAB), ROW_SLAB)
        x = x_ref[pl.ds(x0, ROW_SLAB), :]
        h_nxt[pl.ds(h0, ROW_SLAB), :] = (_rms(x) * gain + shift).astype(h_nxt.dtype)


def _mod_kernel(c_ref, w_ref, b_ref, o_ref):
    o_ref[...] = _dot(_silu(c_ref[...]), w_ref[...]) + b_ref[...]


def _mod_all(c, mod_w, mod_b):
    L, D, N = mod_w.shape
    B = c.shape[0]
    Bp = -(-B // SUBLANES) * SUBLANES
    c_pad = jnp.zeros((Bp, D), F32).at[:B].set(c)
    tn = _tile(N, 2048, LANES)
    return pl.pallas_call(
        _mod_kernel,
        grid=(L, N // tn),
        in_specs=[
            pl.BlockSpec((Bp, D), lambda l, j: (0, 0)),
            pl.BlockSpec((None, D, tn), lambda l, j: (l, 0, j)),
            pl.BlockSpec((None, 1, tn), lambda l, j: (l, 0, j)),
        ],
        out_specs=pl.BlockSpec((None, Bp, tn), lambda l, j: (l, 0, j)),
        out_shape=jax.ShapeDtypeStruct((L, Bp, N), F32),
        compiler_params=_params("parallel", "parallel"),
        name="mod",
    )(c_pad, mod_w, mod_b.reshape(L, 1, N))


def _ffn_kernel(x_ref, mod_ref, g_ref, w1_ref, w3_ref, w2_ref, fg_ref, *rest, final_norm, cast_next):
    if cast_next:
        n1_ref, n3_ref, n2_ref, o_ref, c1_ref, c3_ref, c2_ref, h_ref = rest
    else:
        o_ref, h_ref = rest
    j = pl.program_id(1)

    @pl.when(j == 0)
    def _():
        _norm_mod_rows(x_ref, mod_ref, g_ref, h_ref, copy_ref=o_ref)

    if cast_next:
        c1_ref[...] = n1_ref[...].astype(BF16)
        c3_ref[...] = n3_ref[...].astype(BF16)
        c2_ref[...] = n2_ref[...].astype(BF16)
    h = h_ref[...]
    a = _dot(h, w1_ref[...])
    b = _dot(h, w3_ref[...])
    p = (_silu(a) * b).astype(BF16)
    D = o_ref.shape[1]
    slab = min(D, 2 * MXU_WIDTH)
    for n0 in range(0, D, slab):
        cols = pl.ds(n0, slab)
        o_ref[:, cols] += (0.5 * mod_ref[2:3, cols]) * _dot(p, w2_ref[:, cols])

    if final_norm:
        @pl.when(j == pl.num_programs(1) - 1)
        def _():
            o_ref[...] = _rms(o_ref[...]) * fg_ref[...]


def _ffn(x, mod3, g, w_bf16, w_f32, nxt, final_g, seq, final_norm):
    M, D = x.shape
    w1, w3, w2 = w_bf16
    Fd = w1.shape[-1]
    tm = _tile(seq, 1024, SUBLANES)
    tf = _tile(Fd, 512, LANES)
    ni, nj = M // tm, Fd // tf
    per_b = seq // tm
    in_specs = [
        pl.BlockSpec((tm, D), lambda i, j: (i, 0)),
        pl.BlockSpec((None, 3, D), lambda i, j: (i // per_b, 0, 0)),
        pl.BlockSpec((1, D), lambda i, j: (0, 0)),
        pl.BlockSpec((D, tf), lambda i, j: (0, j)),
        pl.BlockSpec((D, tf), lambda i, j: (0, j)),
        pl.BlockSpec((tf, D), lambda i, j: (j, 0)),
        pl.BlockSpec((1, D), lambda i, j: (0, 0)),
    ]
    out_specs = [pl.BlockSpec((tm, D), lambda i, j: (i, 0))]
    out_shape = [jax.ShapeDtypeStruct((M, D), F32)]
    args = [x, mod3, g.reshape(1, D), w1, w3, w2, final_g.reshape(1, D)]
    if nxt is not None:
        nl, nh = nxt
        dr = D // ni
        assert D % ni == 0 and dr % LANES == 0
        in_specs += [
            pl.BlockSpec((None, None, dr, tf), lambda i, j: (nl, nh, i, j)),
            pl.BlockSpec((None, None, dr, tf), lambda i, j: (nl, nh, i, j)),
            pl.BlockSpec((None, None, tf, dr), lambda i, j: (nl, nh, j, i)),
        ]
        out_specs += [
            pl.BlockSpec((dr, tf), lambda i, j: (i, j)),
            pl.BlockSpec((dr, tf), lambda i, j: (i, j)),
            pl.BlockSpec((tf, dr), lambda i, j: (j, i)),
        ]
        out_shape += [jax.ShapeDtypeStruct((D, Fd), BF16), jax.ShapeDtypeStruct((D, Fd), BF16),
                      jax.ShapeDtypeStruct((Fd, D), BF16)]
        args += list(w_f32)
    out = pl.pallas_call(
        functools.partial(_ffn_kernel, final_norm=final_norm, cast_next=nxt is not None),
        grid=(ni, nj),
        in_specs=in_specs,
        out_specs=out_specs,
        out_shape=out_shape,
        scratch_shapes=[pltpu.VMEM((tm, D), BF16)],
        compiler_params=_params("parallel", "arbitrary"),
        name="ffn",
    )(*args)
    return out[0], tuple(out[1:])


def _gelu(x):
    return 0.5 * x * (1.0 + lax.erf(x * 0.5 ** 0.5))


def _fold_lanes(a):
    out = a[:, 0:LANES]
    for c0 in range(LANES, a.shape[1], LANES):
        out = out + a[:, c0:c0 + LANES]
    return out


def _inproj_a_kernel(x_ref, mod_ref, modn_ref, g_ref, w_ref, o_ref, s1_ref, s2_ref, ha_ref, hb_ref, *,
                     first_v, hide_slabs):
    i, j = pl.program_id(0), pl.program_id(1)
    tm = o_ref.shape[0]

    @pl.when(jnp.logical_and(i == 0, j == 0))
    def _():
        _norm_mod_rows(x_ref, mod_ref, g_ref, ha_ref.at[pl.ds(0, tm), :])

    @pl.when(j == 0)
    def _():
        s1_ref[...] = jnp.zeros_like(s1_ref)
        s2_ref[...] = jnp.zeros_like(s2_ref)

    def step(h_cur, h_nxt):
        _norm_mod_hidden(x_ref, modn_ref, g_ref, h_nxt, hide_slabs)
        y = _gelu(_dot(h_cur[pl.ds(0, tm), :], w_ref[...]))
        o_ref[...] = y.astype(o_ref.dtype)
        is_v = j >= first_v
        s1_ref[...] += jnp.where(is_v, _fold_lanes(y), 0.0)
        s2_ref[...] += jnp.where(is_v, _fold_lanes(y * y), 0.0)

    even = i % 2 == 0
    pl.when(even)(lambda: step(ha_ref, hb_ref))
    pl.when(jnp.logical_not(even))(lambda: step(hb_ref, ha_ref))


def _inproj_a(x, mod3, g, w, seq):
    M, D = x.shape
    N = w.shape[-1]
    tm = _tile(seq, 1024, SUBLANES)
    tn = _tile(N // 2, 2048, LANES)
    per_b = seq // tm
    stat = pl.BlockSpec((tm, LANES), lambda i, j: (i, 0))
    ni, nj = M // tm, N // tn
    hide_slabs, hide_rows = _lookahead_plan(tm, nj)
    return pl.pallas_call(
        functools.partial(_inproj_a_kernel, first_v=N // 2 // tn, hide_slabs=hide_slabs),
        grid=(ni, nj),
        in_specs=[
            _lookahead_x_spec(tm, D, ni),
            pl.BlockSpec((None, 3, D), lambda i, j: (i // per_b, 0, 0)),
            pl.BlockSpec((None, 3, D), lambda i, j: (jnp.minimum(i + 1, ni - 1) // per_b, 0, 0)),
            pl.BlockSpec((1, D), lambda i, j: (0, 0)),
            pl.BlockSpec((D, tn), lambda i, j: (0, j)),
        ],
        out_specs=[pl.BlockSpec((tm, tn), lambda i, j: (i, j)), stat, stat],
        out_shape=[jax.ShapeDtypeStruct((M, N), BF16), jax.ShapeDtypeStruct((M, LANES), F32),
                   jax.ShapeDtypeStruct((M, LANES), F32)],
        scratch_shapes=[pltpu.VMEM((tm + hide_rows, D), BF16), pltpu.VMEM((tm + hide_rows, D), BF16)],
        compiler_params=_params("arbitrary", "arbitrary"),
        name="inproj_a",
    )(x, mod3, mod3, g.reshape(1, D), w)


def _log_forget(x, lbl, row):
    m = jnp.max(lbl, axis=0, keepdims=True)
    e = jnp.exp(lbl - m)
    sm = e / jnp.sum(e, axis=0, keepdims=True)
    cum = sm[0:1]
    for r in range(1, row + 1):
        cum = cum + sm[r:r + 1]
    lb = jnp.maximum(cum - sm[0:1], 0.0)
    t = jnp.exp(-jnp.abs(x))
    pos = x >= 0.0
    num = jnp.log(jnp.where(pos, 1.0 + lb * t, t + lb))
    num = jnp.where(pos, num, jnp.maximum(num, x))
    return num - jnp.log(1.0 + t)


def _inproj_b_kernel(*refs, row, n_cast, hide_slabs):
    x_ref, mod_ref, modn_ref, g_ref, wq_ref, wf_ref, wi_ref, wg_ref, lbl_ref = refs[:9]
    cast_in = refs[9:9 + n_cast]
    q_ref, lf_ref, i_ref, sg_ref = refs[9 + n_cast:13 + n_cast]
    cast_out = refs[13 + n_cast:13 + 2 * n_cast]
    ha_ref, hb_ref = refs[-2:]
    i = pl.program_id(0)
    tm = q_ref.shape[0]

    @pl.when(jnp.logical_and(i == 0, pl.program_id(1) == 0))
    def _():
        _norm_mod_rows(x_ref, mod_ref, g_ref, ha_ref.at[pl.ds(0, tm), :])

    def step(h_cur, h_nxt):
        _norm_mod_hidden(x_ref, modn_ref, g_ref, h_nxt, hide_slabs)
        for src, dst in zip(cast_in, cast_out):
            dst[...] = src[...].astype(BF16)
        h = h_cur[pl.ds(0, tm), :]
        q_ref[...] = _silu(_dot(h, wq_ref[...])).astype(q_ref.dtype)
        lf_ref[...] = _log_forget(_dot(h, wf_ref[...]), lbl_ref[...], row)
        i_ref[...] = _dot(h, wi_ref[...]).astype(i_ref.dtype)
        sg_ref[...] = _silu(_dot(h, wg_ref[...])).astype(sg_ref.dtype)

    even = i % 2 == 0
    pl.when(even)(lambda: step(ha_ref, hb_ref))
    pl.when(jnp.logical_not(even))(lambda: step(hb_ref, ha_ref))


def _inproj_b(x, mod3, g, w, lb_logits, row, seq, casts):
    M, D = x.shape
    R = lb_logits.shape[0]
    tm = _tile(seq, 1024, SUBLANES)
    tn = _tile(D, MXU_WIDTH, LANES)
    per_sec = D // tn
    per_b = seq // tm
    ni, nj = M // tm, per_sec

    def wspec(sec):
        return pl.BlockSpec((D, tn), lambda i, j: (0, sec * per_sec + j))

    out = pl.BlockSpec((tm, tn), lambda i, j: (i, j))
    cast_in, cast_out, cast_shape = [], [], []
    for stack, idx in casts:
        _, A, Bc = stack.shape
        rows = A // (ni * nj)
        assert A % (ni * nj) == 0 and rows % (2 * SUBLANES) == 0
        cast_in.append(pl.BlockSpec((None, rows, Bc), lambda i, j, idx=idx: (idx, i * nj + j, 0)))
        cast_out.append(pl.BlockSpec((rows, Bc), lambda i, j: (i * nj + j, 0)))
        cast_shape.append(jax.ShapeDtypeStruct((A, Bc), BF16))
    hide_slabs, hide_rows = _lookahead_plan(tm, nj)
    res = pl.pallas_call(
        functools.partial(_inproj_b_kernel, row=row, n_cast=len(casts), hide_slabs=hide_slabs),
        grid=(ni, nj),
        in_specs=[
            _lookahead_x_spec(tm, D, ni),
            pl.BlockSpec((None, 3, D), lambda i, j: (i // per_b, 0, 0)),
            pl.BlockSpec((None, 3, D), lambda i, j: (jnp.minimum(i + 1, ni - 1) // per_b, 0, 0)),
            pl.BlockSpec((1, D), lambda i, j: (0, 0)),
            wspec(0), wspec(1), wspec(2), wspec(3),
            pl.BlockSpec((R, tn), lambda i, j: (0, j)),
        ] + cast_in,
        out_specs=[out, out, out, out] + cast_out,
        out_shape=[jax.ShapeDtypeStruct((M, D), BF16), jax.ShapeDtypeStruct((M, D), F32),
                   jax.ShapeDtypeStruct((M, D), BF16), jax.ShapeDtypeStruct((M, D), BF16)] + cast_shape,
        scratch_shapes=[pltpu.VMEM((tm + hide_rows, D), BF16), pltpu.VMEM((tm + hide_rows, D), BF16)],
        compiler_params=_params("arbitrary", "arbitrary"),
        name="inproj_b",
    )(x, mod3, mod3, g.reshape(1, D), w, w, w, w, lb_logits, *[stack for stack, _ in casts])
    return res[:4], res[4:]


def _sgu_outproj_kernel(u_ref, v_ref, s1_ref, s2_ref, lng_ref, lnb_ref, ws_ref, bs_ref, w_ref, x_ref, mod_ref,
                        o_ref, p_ref, *, width):
    tm, D = o_ref.shape

    @pl.when(pl.program_id(1) == 0)
    def _():
        def rows_body(r, carry):
            rows = pl.ds(pl.multiple_of(r * ROW_SLAB, ROW_SLAB), ROW_SLAB)
            o_ref[rows, :] = x_ref[rows, :]
            return carry
        lax.fori_loop(0, tm // ROW_SLAB, rows_body, 0, unroll=4)

    mean = jnp.sum(s1_ref[...], axis=-1, keepdims=True) * (1.0 / width)
    var = jnp.sum(s2_ref[...], axis=-1, keepdims=True) * (1.0 / width) - mean * mean
    rstd = lax.rsqrt(var + EPS)

    t_id = lax.broadcasted_iota(jnp.int32, (A_BLOCK, A_BLOCK), 0) // CHUNK
    s_id = lax.broadcasted_iota(jnp.int32, (A_BLOCK, A_BLOCK), 1) // CHUNK
    w_s = jnp.where(s_id <= t_id, ws_ref[...], 0.0).astype(BF16)
    lng, lnb, bias = lng_ref[...], lnb_ref[...], bs_ref[...]
    gate = mod_ref[2:3, :]
    slab = min(D, 2 * MXU_WIDTH)
    half = max(A_BLOCK, tm // 2)
    for h0 in range(0, tm, half):
        for r0 in range(h0, h0 + half, A_BLOCK):
            rows = pl.ds(r0, A_BLOCK)
            vn = (v_ref[rows, :].astype(F32) - mean[r0:r0 + A_BLOCK]) * rstd[r0:r0 + A_BLOCK] * lng + lnb
            mixed = _dot(w_s, vn.astype(BF16)) + bias
            p_ref[rows, :] = (u_ref[rows, :].astype(F32) * mixed).astype(BF16)
        hrows = pl.ds(h0, half)
        p = p_ref[hrows, :]
        for n0 in range(0, D, slab):
            cols = pl.ds(n0, slab)
            o_ref[hrows, cols] += gate[:, n0:n0 + slab] * _dot(p, w_ref[:, cols])


def _sgu_outproj(z, s1, s2, ln_g, ln_b, w_s, b_s, w, x, mod3, seq):
    M, W2 = z.shape
    W = W2 // 2
    G = w_s.shape[0]
    gd = W // G
    D = w.shape[-1]
    assert gd % LANES == 0
    tm = _tile(seq, 1024, A_BLOCK)
    per_b = seq // tm
    stat = pl.BlockSpec((tm, LANES), lambda i, k: (i, 0))
    return pl.pallas_call(
        functools.partial(_sgu_outproj_kernel, width=W),
        grid=(M // tm, G),
        in_specs=[
            pl.BlockSpec((tm, gd), lambda i, k: (i, k)),
            pl.BlockSpec((tm, gd), lambda i, k: (i, G + k)),
            stat, stat,
            pl.BlockSpec((1, gd), lambda i, k: (0, k)),
            pl.BlockSpec((1, gd), lambda i, k: (0, k)),
            pl.BlockSpec((None, A_BLOCK, A_BLOCK), lambda i, k: (k, 0, 0)),
            pl.BlockSpec((None, A_BLOCK, 1), lambda i, k: (k, 0, 0)),
            pl.BlockSpec((gd, D), lambda i, k: (k, 0)),
            pl.BlockSpec((tm, D), lambda i, k: (i, 0)),
            pl.BlockSpec((None, 3, D), lambda i, k: (i // per_b, 0, 0)),
        ],
        out_specs=pl.BlockSpec((tm, D), lambda i, k: (i, 0)),
        out_shape=jax.ShapeDtypeStruct((M, D), F32),
        scratch_shapes=[pltpu.VMEM((tm, gd), BF16)],
        compiler_params=_params("parallel", "arbitrary"),
        name="sgu_outproj",
    )(z, z, s1, s2, ln_g.reshape(1, W), ln_b.reshape(1, W), w_s, b_s[:, :, None], w, x, mod3)


GROUP_CHUNKS = MXU_WIDTH // CHUNK


def _level_sizes():
    sizes, n = [], CHUNK // 2
    while n >= 1:
        sizes.append(n)
        n //= 2
    return sizes


def _decay_tables():
    L = CHUNK
    sizes = _level_sizes()
    D = np.zeros((2 + len(sizes), L, L), np.float32)
    lid = np.full((L, L), -1, np.int32)
    r = np.arange(L)
    for t in range(L):
        D[0, t] = r <= t
        D[1, t] = r > t
        lid[t, t] = 0
    for li, n in enumerate(sizes):
        for t in range(L):
            m = (t // (2 * n)) * 2 * n + n - 1
            if (t // n) % 2 == 1:
                D[2 + li, t] = (r > m) & (r <= t)
                lid[t, m - n + 1:m + 1] = li + 1
            else:
                D[2 + li, t] = (r > t) & (r <= m)
    D = D.reshape(-1, L)
    return np.concatenate([D, D, D], axis=1), lid


def _split3(x):
    hi = x.astype(BF16)
    r1 = x - hi.astype(F32)
    mid = r1.astype(BF16)
    lo = (r1 - mid.astype(F32)).astype(BF16)
    return jnp.concatenate([hi, mid, lo], axis=0)


def _recur_kernel(q_ref, lf_ref, v_ref, g_ref, ng_ref, dcat_ref, lid_ref, o_ref, s_ref):
    @pl.when(pl.program_id(2) == 0)
    def _():
        s_ref[...] = jnp.zeros_like(s_ref)

    ts, dk = q_ref.shape
    n_chunks = ts // CHUNK
    n_lvl = len(_level_sizes())
    gr = lid_ref.shape[0]
    lid = lid_ref[...]
    ng = ng_ref[...]

    lf_wide = jnp.concatenate([lf_ref[pl.ds(c * CHUNK, CHUNK), :] for c in range(n_chunks)], axis=1)
    e = jnp.exp(_dot(dcat_ref[...], _split3(lf_wide)))

    def decay(b):
        return jnp.concatenate(
            [e[b * CHUNK:(b + 1) * CHUNK, c * dk:(c + 1) * dk] for c in range(n_chunks)], axis=0)

    qb = q_ref[...]
    q = qb.astype(F32)
    k = 1.0 - jnp.exp(lf_ref[...])
    vb = v_ref[...]
    qs = (q * decay(0)).astype(BF16)
    ks = (k * decay(1)).astype(BF16)
    ql = [qb] + [(q * decay(2 + li)).astype(BF16) for li in range(n_lvl)]
    kl = [k.astype(BF16)] + [(k * decay(2 + li)).astype(BF16) for li in range(n_lvl)]

    intra = []
    for g0 in range(0, ts, gr):
        rs = slice(g0, g0 + gr)
        sc = jnp.where(lid == 0, _dot_nt(ql[0][rs], kl[0][rs]), 0.0)
        for li in range(1, n_lvl + 1):
            sc = jnp.where(lid == li, _dot_nt(ql[li][rs], kl[li][rs]), sc)
        intra.append(_dot(sc.astype(BF16), vb[rs]))

    upd = [_dot_tn(vb[r0:r0 + CHUNK], ks[r0:r0 + CHUNK]) for r0 in range(0, ts, CHUNK)]
    states = [s_ref[...]]
    for c in range(n_chunks):
        e_last = e[CHUNK - 1:CHUNK, c * dk:(c + 1) * dk]
        states.append(states[c] * e_last + upd[c])
    s_ref[...] = states[n_chunks]
    for c in range(n_chunks):
        r0 = c * CHUNK
        rows = pl.ds(r0, CHUNK)
        o = _dot_nt(qs[r0:r0 + CHUNK], states[c].astype(BF16)) + intra[r0 // gr][r0 % gr:r0 % gr + CHUNK]
        o_ref[rows, :] = (_rms(o) * ng * g_ref[rows, :].astype(F32)).astype(o_ref.dtype)


def _recurrence(q, lf, iv, sg, norm_g, batch, seq):
    M, D = q.shape
    H, dv = norm_g.shape
    ts = _tile(seq, 4096, CHUNK)
    nt = seq // ts
    group = min(GROUP_CHUNKS, ts // CHUNK)
    assert (ts // CHUNK) % group == 0
    dcat, lid = _decay_tables()
    lid_g = np.full((group * CHUNK, group * CHUNK), -1, np.int32)
    for c in range(group):
        lid_g[c * CHUNK:(c + 1) * CHUNK, c * CHUNK:(c + 1) * CHUNK] = lid

    head = pl.BlockSpec((ts, dv), lambda b, h, t: (b * nt + t, h))
    return pl.pallas_call(
        _recur_kernel,
        grid=(batch, H, nt),
        in_specs=[
            head, head, head, head,
            pl.BlockSpec((None, 1, dv), lambda b, h, t: (h, 0, 0)),
            pl.BlockSpec(dcat.shape, lambda b, h, t: (0, 0)),
            pl.BlockSpec(lid_g.shape, lambda b, h, t: (0, 0)),
        ],
        out_specs=head,
        out_shape=jax.ShapeDtypeStruct((M, D), BF16),
        scratch_shapes=[pltpu.VMEM((dv, dv), F32)],
        compiler_params=_params("parallel", "parallel", "arbitrary"),
        name="recurrence",
    )(q, lf, iv, sg, norm_g.reshape(H, 1, dv), jnp.asarray(dcat, BF16), jnp.asarray(lid_g))


def _outproj_kernel(p_ref, w_ref, x_ref, mod_ref, o_ref):
    D = o_ref.shape[1]
    slab = min(D, 2 * MXU_WIDTH)
    gate = mod_ref[2:3, :]
    p = p_ref[...]
    for n0 in range(0, D, slab):
        cols = pl.ds(n0, slab)
        o_ref[:, cols] = x_ref[:, cols] + gate[:, n0:n0 + slab] * _dot(p, w_ref[:, cols])


def _outproj(p, w, x, mod3, seq):
    M, K = p.shape
    D = w.shape[-1]
    tm = _tile(seq, 512, SUBLANES)
    per_b = seq // tm
    return pl.pallas_call(
        _outproj_kernel,
        grid=(M // tm,),
        in_specs=[
            pl.BlockSpec((tm, K), lambda i: (i, 0)),
            pl.BlockSpec((K, D), lambda i: (0, 0)),
            pl.BlockSpec((tm, D), lambda i: (i, 0)),
            pl.BlockSpec((None, 3, D), lambda i: (i // per_b, 0, 0)),
        ],
        out_specs=pl.BlockSpec((tm, D), lambda i: (i, 0)),
        out_shape=jax.ShapeDtypeStruct((M, D), F32),
        compiler_params=_params("parallel"),
        name="outproj",
    )(p, w, x, mod3)


def kernel(x, c, mod_w, mod_b, norm_g, ffn_w1, ffn_w3, ffn_w2, a_w_in, a_ln_g, a_ln_b, a_w_s, a_b_s, a_w_out,
           b_w_in, b_lb_logits, b_norm_g, b_w_out, final_g):
    B, S, D = x.shape
    depth = mod_w.shape[0]
    n_mixers = 2
    assert S % A_BLOCK == 0 and A_BLOCK % CHUNK == 0 and D % LANES == 0
    ffn_f32 = (ffn_w1, ffn_w3, ffn_w2)
    ffn_order = [(layer, half) for layer in range(depth) for half in range(2)]
    w_ffn = tuple(w[0, 0].astype(BF16) for w in ffn_f32)
    n_a, n_b = a_w_in.shape[0], b_w_in.shape[0]
    a_in, a_out = {0: a_w_in[0].astype(BF16)}, {0: a_w_out[0].astype(BF16)}
    b_in, b_out = ({0: b_w_in[0].astype(BF16)} if n_b else {}), {}

    mod = _mod_all(c, mod_w, mod_b)[:, :B].reshape(depth, B, N_SUB, 3, D)
    xf = x.reshape(B * S, D)

    def ffn(xf, w_ffn, layer, half):
        pos = ffn_order.index((layer, half))
        nxt = ffn_order[pos + 1] if pos + 1 < len(ffn_order) else None
        return _ffn(xf, mod[layer, :, 2 * half], norm_g[layer, 2 * half], w_ffn, ffn_f32, nxt, final_g, S,
                    nxt is None)

    for layer in range(depth):
        j = layer // n_mixers
        xf, w_ffn = ffn(xf, w_ffn, layer, 0)
        if layer % n_mixers == 0:
            z, s1, s2 = _inproj_a(xf, mod[layer, :, 1], norm_g[layer, 1], a_in[j], S)
            xf = _sgu_outproj(z, s1, s2, a_ln_g[j], a_ln_b[j], a_w_s[j], a_b_s[j], a_out[j], xf,
                              mod[layer, :, 1], S)
        else:
            casts, dests = [(b_w_out, j)], [(b_out, j)]
            if j + 1 < n_a:
                casts += [(a_w_in, j + 1), (a_w_out, j + 1)]
                dests += [(a_in, j + 1), (a_out, j + 1)]
            if j + 1 < n_b:
                casts += [(b_w_in, j + 1)]
                dests += [(b_in, j + 1)]
            (q, lf, iv, sg), cast = _inproj_b(xf, mod[layer, :, 1], norm_g[layer, 1], b_in[j], b_lb_logits, j, S,
                                              casts)
            for (table, key), w_cast in zip(dests, cast):
                table[key] = w_cast
            p = _recurrence(q, lf, iv, sg, b_norm_g[j], B, S)
            xf = _outproj(p, b_out[j], xf, mod[layer, :, 1], S)
        xf, w_ffn = ffn(xf, w_ffn, layer, 1)
    return xf.reshape(B, S, D)
```

```python
import functools

import numpy as np
import jax
import jax.numpy as jnp
from jax import lax
from jax.experimental import pallas as pl
from jax.experimental.pallas import tpu as pltpu

EPS = 1e-6
CHUNK = 64
A_BLOCK = 128
N_SUB = 3
LANES = 128
SUBLANES = 8
MXU_WIDTH = 256
ROW_SLAB = 16
VMEM_LIMIT_BYTES = 60 * 1024 * 1024

F32 = jnp.float32
BF16 = jnp.bfloat16


def _tile(dim, target, align):
    if dim <= target:
        return dim
    t = (target // align) * align
    while t >= align:
        if dim % t == 0:
            return t
        t -= align
    return dim


def _params(*semantics):
    return pltpu.CompilerParams(dimension_semantics=semantics, vmem_limit_bytes=VMEM_LIMIT_BYTES)


def _dot(a, b):
    return jnp.dot(a, b, preferred_element_type=F32)


def _dot_nt(a, b):
    return lax.dot_general(a, b, (((1,), (1,)), ((), ())), preferred_element_type=F32)


def _dot_tn(a, b):
    return lax.dot_general(a, b, (((0,), (0,)), ((), ())), preferred_element_type=F32)


def _silu(x):
    return (0.5 * x) * (1.0 + jnp.tanh(0.5 * x))


def _rms(x):
    return x * lax.rsqrt(jnp.mean(x * x, axis=-1, keepdims=True) + EPS)


def _norm_mod_rows(x_ref, mod_ref, g_ref, h_ref, copy_ref=None):
    gain = g_ref[...] * (1.0 + mod_ref[1:2, :])
    shift = mod_ref[0:1, :]

    def rows_body(r, carry):
        rows = pl.ds(pl.multiple_of(r * ROW_SLAB, ROW_SLAB), ROW_SLAB)
        x = x_ref[rows, :]
        h_ref[rows, :] = (_rms(x) * gain + shift).astype(h_ref.dtype)
        if copy_ref is not None:
            copy_ref[rows, :] = x
        return carry

    lax.fori_loop(0, x_ref.shape[0] // ROW_SLAB, rows_body, 0, unroll=8 if copy_ref is not None else 16)


def _mod_kernel(c_ref, w_ref, b_ref, o_ref):
    o_ref[...] = _dot(_silu(c_ref[...]), w_ref[...]) + b_ref[...]


def _mod_all(c, mod_w, mod_b):
    L, D, N = mod_w.shape
    B = c.shape[0]
    Bp = -(-B // SUBLANES) * SUBLANES
    c_pad = jnp.zeros((Bp, D), F32).at[:B].set(c)
    tn = _tile(N, 2048, LANES)
    return pl.pallas_call(
        _mod_kernel,
        grid=(L, N // tn),
        in_specs=[
            pl.BlockSpec((Bp, D), lambda l, j: (0, 0)),
            pl.BlockSpec((None, D, tn), lambda l, j: (l, 0, j)),
            pl.BlockSpec((None, 1, tn), lambda l, j: (l, 0, j)),
        ],
        out_specs=pl.BlockSpec((None, Bp, tn), lambda l, j: (l, 0, j)),
        out_shape=jax.ShapeDtypeStruct((L, Bp, N), F32),
        compiler_params=_params("parallel", "parallel"),
        name="mod",
    )(c_pad, mod_w, mod_b.reshape(L, 1, N))


def _ffn_kernel(x_ref, mod_ref, g_ref, w1_ref, w3_ref, w2_ref, fg_ref, *rest, final_norm, cast_next):
    if cast_next:
        n1_ref, n3_ref, n2_ref, o_ref, c1_ref, c3_ref, c2_ref, h_ref = rest
    else:
        o_ref, h_ref = rest
    j = pl.program_id(1)

    @pl.when(j == 0)
    def _():
        _norm_mod_rows(x_ref, mod_ref, g_ref, h_ref, copy_ref=o_ref)

    if cast_next:
        c1_ref[...] = n1_ref[...].astype(BF16)
        c3_ref[...] = n3_ref[...].astype(BF16)
        c2_ref[...] = n2_ref[...].astype(BF16)
    h = h_ref[...]
    a = _dot(h, w1_ref[...])
    b = _dot(h, w3_ref[...])
    p = (_silu(a) * b).astype(BF16)
    D = o_ref.shape[1]
    slab = min(D, 2 * MXU_WIDTH)
    for n0 in range(0, D, slab):
        cols = pl.ds(n0, slab)
        o_ref[:, cols] += (0.5 * mod_ref[2:3, cols]) * _dot(p, w2_ref[:, cols])

    if final_norm:
        @pl.when(j == pl.num_programs(1) - 1)
        def _():
            o_ref[...] = _rms(o_ref[...]) * fg_ref[...]


def _ffn(x, mod3, g, w_bf16, w_f32, nxt, final_g, seq, final_norm):
    M, D = x.shape
    w1, w3, w2 = w_bf16
    Fd = w1.shape[-1]
    tm = _tile(seq, 1024, SUBLANES)
    tf = _tile(Fd, 512, LANES)
    ni, nj = M // tm, Fd // tf
    per_b = seq // tm
    in_specs = [
        pl.BlockSpec((tm, D), lambda i, j: (i, 0)),
        pl.BlockSpec((None, 3, D), lambda i, j: (i // per_b, 0, 0)),
        pl.BlockSpec((1, D), lambda i, j: (0, 0)),
        pl.BlockSpec((D, tf), lambda i, j: (0, j)),
        pl.BlockSpec((D, tf), lambda i, j: (0, j)),
        pl.BlockSpec((tf, D), lambda i, j: (j, 0)),
        pl.BlockSpec((1, D), lambda i, j: (0, 0)),
    ]
    out_specs = [pl.BlockSpec((tm, D), lambda i, j: (i, 0))]
    out_shape = [jax.ShapeDtypeStruct((M, D), F32)]
    args = [x, mod3, g.reshape(1, D), w1, w3, w2, final_g.reshape(1, D)]
    if nxt is not None:
        nl, nh = nxt
        dr = D // ni
        assert D % ni == 0 and dr % LANES == 0
        in_specs += [
            pl.BlockSpec((None, None, dr, tf), lambda i, j: (nl, nh, i, j)),
            pl.BlockSpec((None, None, dr, tf), lambda i, j: (nl, nh, i, j)),
            pl.BlockSpec((None, None, tf, dr), lambda i, j: (nl, nh, j, i)),
        ]
        out_specs += [
            pl.BlockSpec((dr, tf), lambda i, j: (i, j)),
            pl.BlockSpec((dr, tf), lambda i, j: (i, j)),
            pl.BlockSpec((tf, dr), lambda i, j: (j, i)),
        ]
        out_shape += [jax.ShapeDtypeStruct((D, Fd), BF16), jax.ShapeDtypeStruct((D, Fd), BF16),
                      jax.ShapeDtypeStruct((Fd, D), BF16)]
        args += list(w_f32)
    out = pl.pallas_call(
        functools.partial(_ffn_kernel, final_norm=final_norm, cast_next=nxt is not None),
        grid=(ni, nj),
        in_specs=in_specs,
        out_specs=out_specs,
        out_shape=out_shape,
        scratch_shapes=[pltpu.VMEM((tm, D), BF16)],
        compiler_params=_params("parallel", "arbitrary"),
        name="ffn",
    )(*args)
    return out[0], tuple(out[1:])


def _gelu(x):
    return 0.5 * x * (1.0 + lax.erf(x * 0.5 ** 0.5))


def _fold_lanes(a):
    out = a[:, 0:LANES]
    for c0 in range(LANES, a.shape[1], LANES):
        out = out + a[:, c0:c0 + LANES]
    return out


def _inproj_a_kernel(x_ref, mod_ref, g_ref, w_ref, o_ref, s1_ref, s2_ref, h_ref, *, first_v):
    j = pl.program_id(1)

    @pl.when(j == 0)
    def _():
        _norm_mod_rows(x_ref, mod_ref, g_ref, h_ref)
        s1_ref[...] = jnp.zeros_like(s1_ref)
        s2_ref[...] = jnp.zeros_like(s2_ref)

    y = _gelu(_dot(h_ref[...], w_ref[...]))
    o_ref[...] = y.astype(o_ref.dtype)
    is_v = j >= first_v
    s1_ref[...] += jnp.where(is_v, _fold_lanes(y), 0.0)
    s2_ref[...] += jnp.where(is_v, _fold_lanes(y * y), 0.0)


def _inproj_a(x, mod3, g, w, seq):
    M, D = x.shape
    N = w.shape[-1]
    tm = _tile(seq, 1024, SUBLANES)
    tn = _tile(N // 2, 2048, LANES)
    per_b = seq // tm
    stat = pl.BlockSpec((tm, LANES), lambda i, j: (i, 0))
    return pl.pallas_call(
        functools.partial(_inproj_a_kernel, first_v=N // 2 // tn),
        grid=(M // tm, N // tn),
        in_specs=[
            pl.BlockSpec((tm, D), lambda i, j: (i, 0)),
            pl.BlockSpec((None, 3, D), lambda i, j: (i // per_b, 0, 0)),
            pl.BlockSpec((1, D), lambda i, j: (0, 0)),
            pl.BlockSpec((D, tn), lambda i, j: (0, j)),
        ],
        out_specs=[pl.BlockSpec((tm, tn), lambda i, j: (i, j)), stat, stat],
        out_shape=[jax.ShapeDtypeStruct((M, N), BF16), jax.ShapeDtypeStruct((M, LANES), F32),
                   jax.ShapeDtypeStruct((M, LANES), F32)],
        scratch_shapes=[pltpu.VMEM((tm, D), BF16)],
        compiler_params=_params("parallel", "arbitrary"),
        name="inproj_a",
    )(x, mod3, g.reshape(1, D), w)


def _log_forget(x, lbl, row):
    m = jnp.max(lbl, axis=0, keepdims=True)
    e = jnp.exp(lbl - m)
    sm = e / jnp.sum(e, axis=0, keepdims=True)
    cum = sm[0:1]
    for r in range(1, row + 1):
        cum = cum + sm[r:r + 1]
    lb = jnp.maximum(cum - sm[0:1], 0.0)
    t = jnp.exp(-jnp.abs(x))
    pos = x >= 0.0
    num = jnp.log(jnp.where(pos, 1.0 + lb * t, t + lb))
    num = jnp.where(pos, num, jnp.maximum(num, x))
    return num - jnp.log(1.0 + t)


def _inproj_b_kernel(*refs, row, n_cast):
    x_ref, mod_ref, g_ref, wq_ref, wf_ref, wi_ref, wg_ref, lbl_ref = refs[:8]
    cast_in = refs[8:8 + n_cast]
    q_ref, lf_ref, i_ref, sg_ref = refs[8 + n_cast:12 + n_cast]
    cast_out = refs[12 + n_cast:12 + 2 * n_cast]
    h_ref = refs[-1]

    @pl.when(pl.program_id(1) == 0)
    def _():
        _norm_mod_rows(x_ref, mod_ref, g_ref, h_ref)

    for src, dst in zip(cast_in, cast_out):
        dst[...] = src[...].astype(BF16)
    h = h_ref[...]
    q_ref[...] = _silu(_dot(h, wq_ref[...])).astype(q_ref.dtype)
    lf_ref[...] = _log_forget(_dot(h, wf_ref[...]), lbl_ref[...], row)
    i_ref[...] = _dot(h, wi_ref[...]).astype(i_ref.dtype)
    sg_ref[...] = _silu(_dot(h, wg_ref[...])).astype(sg_ref.dtype)


def _inproj_b(x, mod3, g, w, lb_logits, row, seq, casts):
    M, D = x.shape
    R = lb_logits.shape[0]
    tm = _tile(seq, 1024, SUBLANES)
    tn = _tile(D, MXU_WIDTH, LANES)
    per_sec = D // tn
    per_b = seq // tm
    ni, nj = M // tm, per_sec

    def wspec(sec):
        return pl.BlockSpec((D, tn), lambda i, j: (0, sec * per_sec + j))

    out = pl.BlockSpec((tm, tn), lambda i, j: (i, j))
    cast_in, cast_out, cast_shape = [], [], []
    for stack, idx in casts:
        _, A, Bc = stack.shape
        rows = A // (ni * nj)
        assert A % (ni * nj) == 0 and rows % (2 * SUBLANES) == 0
        cast_in.append(pl.BlockSpec((None, rows, Bc), lambda i, j, idx=idx: (idx, i * nj + j, 0)))
        cast_out.append(pl.BlockSpec((rows, Bc), lambda i, j: (i * nj + j, 0)))
        cast_shape.append(jax.ShapeDtypeStruct((A, Bc), BF16))
    res = pl.pallas_call(
        functools.partial(_inproj_b_kernel, row=row, n_cast=len(casts)),
        grid=(ni, nj),
        in_specs=[
            pl.BlockSpec((tm, D), lambda i, j: (i, 0)),
            pl.BlockSpec((None, 3, D), lambda i, j: (i // per_b, 0, 0)),
            pl.BlockSpec((1, D), lambda i, j: (0, 0)),
            wspec(0), wspec(1), wspec(2), wspec(3),
            pl.BlockSpec((R, tn), lambda i, j: (0, j)),
        ] + cast_in,
        out_specs=[out, out, out, out] + cast_out,
        out_shape=[jax.ShapeDtypeStruct((M, D), BF16), jax.ShapeDtypeStruct((M, D), F32),
                   jax.ShapeDtypeStruct((M, D), BF16), jax.ShapeDtypeStruct((M, D), BF16)] + cast_shape,
        scratch_shapes=[pltpu.VMEM((tm, D), BF16)],
        compiler_params=_params("parallel", "arbitrary"),
        name="inproj_b",
    )(x, mod3, g.reshape(1, D), w, w, w, w, lb_logits, *[stack for stack, _ in casts])
    return res[:4], res[4:]


def _sgu_outproj_kernel(u_ref, v_ref, s1_ref, s2_ref, lng_ref, lnb_ref, ws_ref, bs_ref, w_ref, x_ref, mod_ref,
                        o_ref, p_ref, *, width):
    tm, D = o_ref.shape

    @pl.when(pl.program_id(1) == 0)
    def _():
        def rows_body(r, carry):
            rows = pl.ds(pl.multiple_of(r * ROW_SLAB, ROW_SLAB), ROW_SLAB)
            o_ref[rows, :] = x_ref[rows, :]
            return carry
        lax.fori_loop(0, tm // ROW_SLAB, rows_body, 0, unroll=4)

    mean = jnp.sum(s1_ref[...], axis=-1, keepdims=True) * (1.0 / width)
    var = jnp.sum(s2_ref[...], axis=-1, keepdims=True) * (1.0 / width) - mean * mean
    rstd = lax.rsqrt(var + EPS)

    t_id = lax.broadcasted_iota(jnp.int32, (A_BLOCK, A_BLOCK), 0) // CHUNK
    s_id = lax.broadcasted_iota(jnp.int32, (A_BLOCK, A_BLOCK), 1) // CHUNK
    w_s = jnp.where(s_id <= t_id, ws_ref[...], 0.0).astype(BF16)
    lng, lnb, bias = lng_ref[...], lnb_ref[...], bs_ref[...]
    gate = mod_ref[2:3, :]
    slab = min(D, 2 * MXU_WIDTH)
    half = max(A_BLOCK, tm // 2)
    for h0 in range(0, tm, half):
        for r0 in range(h0, h0 + half, A_BLOCK):
            rows = pl.ds(r0, A_BLOCK)
            vn = (v_ref[rows, :].astype(F32) - mean[r0:r0 + A_BLOCK]) * rstd[r0:r0 + A_BLOCK] * lng + lnb
            mixed = _dot(w_s, vn.astype(BF16)) + bias
            p_ref[rows, :] = (u_ref[rows, :].astype(F32) * mixed).astype(BF16)
        hrows = pl.ds(h0, half)
        p = p_ref[hrows, :]
        for n0 in range(0, D, slab):
            cols = pl.ds(n0, slab)
            o_ref[hrows, cols] += gate[:, n0:n0 + slab] * _dot(p, w_ref[:, cols])


def _sgu_outproj(z, s1, s2, ln_g, ln_b, w_s, b_s, w, x, mod3, seq):
    M, W2 = z.shape
    W = W2 // 2
    G = w_s.shape[0]
    gd = W // G
    D = w.shape[-1]
    assert gd % LANES == 0
    tm = _tile(seq, 1024, A_BLOCK)
    per_b = seq // tm
    stat = pl.BlockSpec((tm, LANES), lambda i, k: (i, 0))
    return pl.pallas_call(
        functools.partial(_sgu_outproj_kernel, width=W),
        grid=(M // tm, G),
        in_specs=[
            pl.BlockSpec((tm, gd), lambda i, k: (i, k)),
            pl.BlockSpec((tm, gd), lambda i, k: (i, G + k)),
            stat, stat,
            pl.BlockSpec((1, gd), lambda i, k: (0, k)),
            pl.BlockSpec((1, gd), lambda i, k: (0, k)),
            pl.BlockSpec((None, A_BLOCK, A_BLOCK), lambda i, k: (k, 0, 0)),
            pl.BlockSpec((None, A_BLOCK, 1), lambda i, k: (k, 0, 0)),
            pl.BlockSpec((gd, D), lambda i, k: (k, 0)),
            pl.BlockSpec((tm, D), lambda i, k: (i, 0)),
            pl.BlockSpec((None, 3, D), lambda i, k: (i // per_b, 0, 0)),
        ],
        out_specs=pl.BlockSpec((tm, D), lambda i, k: (i, 0)),
        out_shape=jax.ShapeDtypeStruct((M, D), F32),
        scratch_shapes=[pltpu.VMEM((tm, gd), BF16)],
        compiler_params=_params("parallel", "arbitrary"),
        name="sgu_outproj",
    )(z, z, s1, s2, ln_g.reshape(1, W), ln_b.reshape(1, W), w_s, b_s[:, :, None], w, x, mod3)


GROUP_CHUNKS = MXU_WIDTH // CHUNK


def _level_sizes():
    sizes, n = [], CHUNK // 2
    while n >= 1:
        sizes.append(n)
        n //= 2
    return sizes


def _decay_tables():
    L = CHUNK
    sizes = _level_sizes()
    D = np.zeros((2 + len(sizes) - 1, L, L), np.float32)
    lid = np.full((L, L), -1, np.int32)
    r = np.arange(L)
    for t in range(L):
        D[0, t] = r <= t
        D[1, t] = r > t
        lid[t, t] = 0
    for li, n in enumerate(sizes):
        for t in range(L):
            m = (t // (2 * n)) * 2 * n + n - 1
            if (t // n) % 2 == 1:
                lid[t, m - n + 1:m + 1] = li + 1
                if n > 1:
                    D[2 + li, t] = (r > m) & (r <= t)
            elif n > 1:
                D[2 + li, t] = (r > t) & (r <= m)
    D = D.reshape(-1, L)
    return np.concatenate([D, D, D], axis=1), lid


def _split3(x):
    hi = x.astype(BF16)
    r1 = x - hi.astype(F32)
    mid = r1.astype(BF16)
    lo = (r1 - mid.astype(F32)).astype(BF16)
    return jnp.concatenate([hi, mid, lo], axis=0)


def _recur_kernel(q_ref, lf_ref, v_ref, g_ref, ng_ref, dcat_ref, lid_ref, o_ref, s_ref):
    @pl.when(pl.program_id(2) == 0)
    def _():
        s_ref[...] = jnp.zeros_like(s_ref)

    ts, dk = q_ref.shape
    n_chunks = ts // CHUNK
    n_lvl = len(_level_sizes())
    gr = lid_ref.shape[0]
    lid = lid_ref[...]
    ng = ng_ref[...]

    lf_wide = jnp.concatenate([lf_ref[pl.ds(c * CHUNK, CHUNK), :] for c in range(n_chunks)], axis=1)
    e = jnp.exp(_dot(dcat_ref[...], _split3(lf_wide)))

    def decay(b):
        return jnp.concatenate(
            [e[b * CHUNK:(b + 1) * CHUNK, c * dk:(c + 1) * dk] for c in range(n_chunks)], axis=0)

    qb = q_ref[...]
    q = qb.astype(F32)
    f = jnp.exp(lf_ref[...])
    k = 1.0 - f
    vb = v_ref[...]
    qs = (q * decay(0)).astype(BF16)
    ks = (k * decay(1)).astype(BF16)
    odd_row = lax.broadcasted_iota(jnp.int32, f.shape, 0) % 2 == 1
    d1 = jnp.where(odd_row, f, 1.0)
    ql = [qb] + [(q * decay(2 + li)).astype(BF16) for li in range(n_lvl - 1)] + [(q * d1).astype(BF16)]
    kl = ([k.astype(BF16)] + [(k * decay(2 + li)).astype(BF16) for li in range(n_lvl - 1)]
          + [(k * d1).astype(BF16)])

    intra = []
    for g0 in range(0, ts, gr):
        rs = slice(g0, g0 + gr)
        sc = jnp.where(lid == 0, _dot_nt(ql[0][rs], kl[0][rs]), 0.0)
        for li in range(1, n_lvl + 1):
            sc = jnp.where(lid == li, _dot_nt(ql[li][rs], kl[li][rs]), sc)
        intra.append(_dot(sc.astype(BF16), vb[rs]))

    upd = [_dot_tn(vb[r0:r0 + CHUNK], ks[r0:r0 + CHUNK]) for r0 in range(0, ts, CHUNK)]
    states = [s_ref[...]]
    for c in range(n_chunks):
        e_last = e[CHUNK - 1:CHUNK, c * dk:(c + 1) * dk]
        states.append(states[c] * e_last + upd[c])
    s_ref[...] = states[n_chunks]
    for c in range(n_chunks):
        r0 = c * CHUNK
        rows = pl.ds(r0, CHUNK)
        o = _dot_nt(qs[r0:r0 + CHUNK], states[c].astype(BF16)) + intra[r0 // gr][r0 % gr:r0 % gr + CHUNK]
        o_ref[rows, :] = (_rms(o) * ng * g_ref[rows, :].astype(F32)).astype(o_ref.dtype)


def _recurrence(q, lf, iv, sg, norm_g, batch, seq):
    M, D = q.shape
    H, dv = norm_g.shape
    ts = _tile(seq, 4096, CHUNK)
    nt = seq // ts
    group = min(GROUP_CHUNKS, ts // CHUNK)
    assert (ts // CHUNK) % group == 0
    dcat, lid = _decay_tables()
    lid_g = np.full((group * CHUNK, group * CHUNK), -1, np.int32)
    for c in range(group):
        lid_g[c * CHUNK:(c + 1) * CHUNK, c * CHUNK:(c + 1) * CHUNK] = lid

    head = pl.BlockSpec((ts, dv), lambda b, h, t: (b * nt + t, h))
    return pl.pallas_call(
        _recur_kernel,
        grid=(batch, H, nt),
        in_specs=[
            head, head, head, head,
            pl.BlockSpec((None, 1, dv), lambda b, h, t: (h, 0, 0)),
            pl.BlockSpec(dcat.shape, lambda b, h, t: (0, 0)),
            pl.BlockSpec(lid_g.shape, lambda b, h, t: (0, 0)),
        ],
        out_specs=head,
        out_shape=jax.ShapeDtypeStruct((M, D), BF16),
        scratch_shapes=[pltpu.VMEM((dv, dv), F32)],
        compiler_params=_params("parallel", "parallel", "arbitrary"),
        name="recurrence",
    )(q, lf, iv, sg, norm_g.reshape(H, 1, dv), jnp.asarray(dcat, BF16), jnp.asarray(lid_g))


def _outproj_kernel(p_ref, w_ref, x_ref, mod_ref, o_ref):
    D = o_ref.shape[1]
    slab = min(D, 2 * MXU_WIDTH)
    gate = mod_ref[2:3, :]
    p = p_ref[...]
    for n0 in range(0, D, slab):
        cols = pl.ds(n0, slab)
        o_ref[:, cols] = x_ref[:, cols] + gate[:, n0:n0 + slab] * _dot(p, w_ref[:, cols])


def _outproj(p, w, x, mod3, seq):
    M, K = p.shape
    D = w.shape[-1]
    tm = _tile(seq, 512, SUBLANES)
    per_b = seq // tm
    return pl.pallas_call(
        _outproj_kernel,
        grid=(M // tm,),
        in_specs=[
            pl.BlockSpec((tm, K), lambda i: (i, 0)),
            pl.BlockSpec((K, D), lambda i: (0, 0)),
            pl.BlockSpec((tm, D), lambda i: (i, 0)),
            pl.BlockSpec((None, 3, D), lambda i: (i // per_b, 0, 0)),
        ],
        out_specs=pl.BlockSpec((tm, D), lambda i: (i, 0)),
        out_shape=jax.ShapeDtypeStruct((M, D), F32),
        compiler_params=_params("parallel"),
        name="outproj",
    )(p, w, x, mod3)


def kernel(x, c, mod_w, mod_b, norm_g, ffn_w1, ffn_w3, ffn_w2, a_w_in, a_ln_g, a_ln_b, a_w_s, a_b_s, a_w_out,
           b_w_in, b_lb_logits, b_norm_g, b_w_out, final_g):
    B, S, D = x.shape
    depth = mod_w.shape[0]
    n_mixers = 2
    assert S % A_BLOCK == 0 and A_BLOCK % CHUNK == 0 and D % LANES == 0
    ffn_f32 = (ffn_w1, ffn_w3, ffn_w2)
    ffn_order = [(layer, half) for layer in range(depth) for half in range(2)]
    w_ffn = tuple(w[0, 0].astype(BF16) for w in ffn_f32)
    n_a, n_b = a_w_in.shape[0], b_w_in.shape[0]
    a_in, a_out = {0: a_w_in[0].astype(BF16)}, {0: a_w_out[0].astype(BF16)}
    b_in, b_out = ({0: b_w_in[0].astype(BF16)} if n_b else {}), {}

    mod = _mod_all(c, mod_w, mod_b)[:, :B].reshape(depth, B, N_SUB, 3, D)
    xf = x.reshape(B * S, D)

    def ffn(xf, w_ffn, layer, half):
        pos = ffn_order.index((layer, half))
        nxt = ffn_order[pos + 1] if pos + 1 < len(ffn_order) else None
        return _ffn(xf, mod[layer, :, 2 * half], norm_g[layer, 2 * half], w_ffn, ffn_f32, nxt, final_g, S,
                    nxt is None)

    for layer in range(depth):
        j = layer // n_mixers
        xf, w_ffn = ffn(xf, w_ffn, layer, 0)
        if layer % n_mixers == 0:
            z, s1, s2 = _inproj_a(xf, mod[layer, :, 1], norm_g[layer, 1], a_in[j], S)
            xf = _sgu_outproj(z, s1, s2, a_ln_g[j], a_ln_b[j], a_w_s[j], a_b_s[j], a_out[j], xf,
                              mod[layer, :, 1], S)
        else:
            casts, dests = [(b_w_out, j)], [(b_out, j)]
            if j + 1 < n_a:
                casts += [(a_w_in, j + 1), (a_w_out, j + 1)]
                dests += [(a_in, j + 1), (a_out, j + 1)]
            if j + 1 < n_b:
                casts += [(b_w_in, j + 1)]
                dests += [(b_in, j + 1)]
            (q, lf, iv, sg), cast = _inproj_b(xf, mod[layer, :, 1], norm_g[layer, 1], b_in[j], b_lb_logits, j, S,
                                              casts)
            for (table, key), w_cast in zip(dests, cast):
                table[key] = w_cast
            p = _recurrence(q, lf, iv, sg, b_norm_g[j], B, S)
            xf = _outproj(p, b_out[j], xf, mod[layer, :, 1], S)
        xf, w_ffn = ffn(xf, w_ffn, layer, 1)
    return xf.reshape(B, S, D)
```
